```python
import jax
import jax.numpy as jnp
from jax import lax
import numpy as np

D_MODEL = 1024
BATCH = 4
SEQ = 8192
DEPTH = 1
DEC_BATCH = 16
DEC_SEQ = 2048
PAST_LEN = 128

F32 = jnp.float32
HEAD_DIM = 64
MIX_WIDTH = D_MODEL
N_HEADS_A = MIX_WIDTH // (2 * HEAD_DIM)
WIDTH_A = N_HEADS_A * HEAD_DIM
N_HEADS_B = MIX_WIDTH // (2 * HEAD_DIM)
N_KV_B = N_HEADS_B // 4
WIDTH_B = N_HEADS_B * HEAD_DIM
KV_WIDTH_B = N_KV_B * HEAD_DIM
IN_COLS = 3 * WIDTH_A + WIDTH_B + 2 * KV_WIDTH_B
SPLITS = (WIDTH_A, 2 * WIDTH_A, 3 * WIDTH_A, 3 * WIDTH_A + WIDTH_B, 3 * WIDTH_A + WIDTH_B + KV_WIDTH_B)
DILATED_PATTERNS = ((128, 1), (512, 4), (2048, 16))
ROPE_THETA = 500000.0
ROPE_DIMS_A = HEAD_DIM // 4
AXIAL_THETA = 10000.0
GRID_W = 64
Q_BLOCK = 128
N_GROUPS = 4
EXPERTS_PER_GROUP = 8
N_EXPERTS = N_GROUPS * EXPERTS_PER_GROUP
TOP_K_IN_GROUP = 2
D_EXPERT = D_MODEL // 2
MOE_BLOCK = 128
EPS = 1e-6
NEG = -1e30

kernel_name = 'hybrid_dilated_axial_hmoe_encoder'


def rmsnorm(x, g):
    xf = x.astype(F32)
    y = xf * lax.rsqrt(jnp.mean(xf * xf, axis=-1, keepdims=True) + EPS)
    return (y * g.astype(F32)).astype(x.dtype)


def rope_tables(pos, n_dims, theta):
    inv = theta ** (-jnp.arange(0, n_dims, 2, dtype=F32) / n_dims)
    ang = pos.astype(F32)[:, None] * inv[None, :]
    return jnp.cos(ang), jnp.sin(ang)


def apply_rope(x, cos, sin):
    n2 = x.shape[-1] // 2
    x1 = x[..., :n2].astype(F32)
    x2 = x[..., n2:].astype(F32)
    c = cos[:, None, :]
    s = sin[:, None, :]
    return jnp.concatenate([x1 * c - x2 * s, x1 * s + x2 * c], axis=-1).astype(x.dtype)


def partial_rope(x, cos, sin):
    return jnp.concatenate([apply_rope(x[..., :ROPE_DIMS_A], cos, sin), x[..., ROPE_DIMS_A:]], axis=-1)


def axial_rope(x, cos_r, sin_r, cos_c, sin_c):
    half = HEAD_DIM // 2
    return jnp.concatenate([apply_rope(x[..., :half], cos_r, sin_r), apply_rope(x[..., half:], cos_c, sin_c)], axis=-1)


def dilated_branch(q, k, v, window, dil):
    B, S, H, Dh = q.shape
    W = window // (2 * dil)
    L = S // dil
    nb = -(-L // W)
    Lp = nb * W
    qr = q.reshape(B, L, dil, H, Dh)
    kr = k.reshape(B, L, dil, H, Dh)
    vr = v.reshape(B, L, dil, H, Dh)
    qb = jnp.pad(qr, ((0, 0), (0, Lp - L), (0, 0), (0, 0), (0, 0))).reshape(B, nb, W, dil, H, Dh)
    pad_kv = ((0, 0), (W, Lp - L + W), (0, 0), (0, 0), (0, 0))
    kp = jnp.pad(kr, pad_kv)
    vp = jnp.pad(vr, pad_kv)

    def bands(a):
        return jnp.concatenate([a[:, o:o + Lp].reshape(B, nb, W, dil, H, Dh) for o in (0, W, 2 * W)], axis=2)

    kb = bands(kp)
    vb = bands(vp)
    i = jnp.arange(W)[:, None]
    m = jnp.arange(3 * W)[None, :]
    delta = m - W - i
    jk = jnp.arange(nb)[:, None, None] * W - W + m[None]
    valid = (jnp.abs(delta) <= W)[None] & (jk >= 0) & (jk < L)
    s = jnp.einsum('bnqrhd,bnkrhd->bnrhqk', qb, kb, preferred_element_type=F32) * (Dh ** -0.5)
    s = jnp.where(valid[None, :, None, None], s, NEG)
    mx = jnp.max(s, axis=-1, keepdims=True)
    p = jnp.exp(s - mx)
    den = jnp.sum(p, axis=-1, keepdims=True)
    o = jnp.einsum('bnrhqk,bnkrhd->bnqrhd', (p / den).astype(v.dtype), vb, preferred_element_type=F32)
    lse = (mx + jnp.log(den))[..., 0]
    o = o.reshape(B, Lp, dil, H, Dh)[:, :L].reshape(B, S, H, Dh)
    lse = jnp.transpose(lse, (0, 1, 4, 2, 3)).reshape(B, Lp, dil, H)[:, :L].reshape(B, S, H)
    return o, lse


def mixer_dilated(q, k, v):
    B, S, H, Dh = q.shape
    outs, lses = [], []
    for window, dil in DILATED_PATTERNS:
        o, l = dilated_branch(q, k, v, window, dil)
        outs.append(o)
        lses.append(l)
    w = jax.nn.softmax(jnp.stack(lses, 0), axis=0)
    o = jnp.sum(w[..., None] * jnp.stack(outs, 0), axis=0)
    return o.reshape(B, S, H * Dh).astype(q.dtype)


def mixer_axial_gqa(q, k, v):
    B, S, Hq, Dh = q.shape
    Hkv = k.shape[2]
    G = Hq // Hkv
    nqb = S // Q_BLOCK
    qb = jnp.moveaxis(q.reshape(B, nqb, Q_BLOCK, Hkv, G, Dh), 1, 0)
    scale = Dh ** -0.5

    def block(qblk):
        s = jnp.einsum('bqkgd,bskd->bkgqs', qblk, k, preferred_element_type=F32) * scale
        p = jax.nn.softmax(s, axis=-1).astype(v.dtype)
        return jnp.einsum('bkgqs,bskd->bqkgd', p, v, preferred_element_type=F32).astype(q.dtype)

    o = lax.map(block, qb)
    return jnp.moveaxis(o, 0, 1).reshape(B, S, Hq * Dh)


def moe_dispatch(xs, expert, gate, w1, w3, w2):
    T, D = xs.shape
    K = expert.shape[1]
    A = T * K
    flat_e = expert.reshape(-1)
    flat_tok = jnp.repeat(jnp.arange(T, dtype=jnp.int32), K)
    flat_w = gate.reshape(-1).astype(F32)
    order = jnp.argsort(flat_e)
    se = flat_e[order]
    counts = jnp.bincount(flat_e, length=N_EXPERTS)
    starts = jnp.cumsum(counts) - counts
    pcounts = (counts + MOE_BLOCK - 1) // MOE_BLOCK * MOE_BLOCK
    pends = jnp.cumsum(pcounts)
    pstarts = pends - pcounts
    dest = pstarts[se] + (jnp.arange(A) - starts[se])
    n_blocks = -(-A // MOE_BLOCK) + N_EXPERTS
    P = n_blocks * MOE_BLOCK
    slot_tok = jnp.full((P,), T, jnp.int32).at[dest].set(flat_tok[order])
    slot_w = jnp.zeros((P,), F32).at[dest].set(flat_w[order])
    block_e = jnp.minimum(jnp.searchsorted(pends, jnp.arange(n_blocks) * MOE_BLOCK, side='right'), N_EXPERTS - 1)
    xpad = jnp.concatenate([xs, jnp.zeros((1, D), xs.dtype)], axis=0)
    xb = xpad[slot_tok].reshape(n_blocks, MOE_BLOCK, D)

    def run(args):
        xblk, e = args
        a = jnp.dot(xblk, w1[e])
        b = jnp.dot(xblk, w3[e])
        return jnp.dot(jax.nn.silu(a) * b, w2[e])

    yb = lax.map(run, (xb, block_e)).reshape(P, D)
    out = jnp.zeros((T + 1, D), F32).at[slot_tok].add(yb.astype(F32) * slot_w[:, None])
    return out[:T].astype(xs.dtype)


def hier_moe(h, w_rg, b_rg, w_re, b_re, w1, w3, w2):
    B, S, D = h.shape
    T = B * S
    xs = h.reshape(T, D)
    g_logits = jnp.dot(xs, w_rg, preferred_element_type=F32) + b_rg.astype(F32)
    g_prob = jax.nn.softmax(g_logits, axis=-1)
    g_sel = jnp.argmax(g_logits, axis=-1)
    g_w = jnp.take_along_axis(g_prob, g_sel[:, None], axis=-1)
    e_logits = (jnp.dot(xs, w_re, preferred_element_type=F32) + b_re.astype(F32)).reshape(T, N_GROUPS, EXPERTS_PER_GROUP)
    e_logits = jnp.take_along_axis(e_logits, g_sel[:, None, None], axis=1)[:, 0]
    e_prob = jax.nn.softmax(e_logits, axis=-1)
    top_p, top_i = lax.top_k(e_prob, TOP_K_IN_GROUP)
    gate = g_w * top_p / jnp.sum(top_p, axis=-1, keepdims=True)
    expert = g_sel[:, None].astype(jnp.int32) * EXPERTS_PER_GROUP + top_i.astype(jnp.int32)
    return moe_dispatch(xs, expert, gate, w1, w3, w2).reshape(B, S, D)


def encoder_layer(x, c, ln1_g, ln2_g, w_ada, b_ada, w_in, w_out, qn_g, kn_g, on_a_g, on_b_g,
                  w_rg, b_rg, w_re, b_re, w1, w3, w2):
    B, S, _ = x.shape
    mod = (jnp.dot(jax.nn.silu(c), w_ada, preferred_element_type=F32) + b_ada.astype(F32)).astype(x.dtype)[:, None, :]
    sh1, sc1, g1, sh2, sc2, g2 = jnp.split(mod, 6, axis=-1)
    h = rmsnorm(x, ln1_g) * (1 + sc1) + sh1
    proj = jnp.dot(h, w_in)
    qa, ka, va, qb, kb, vb = jnp.split(proj, SPLITS, axis=-1)
    t = jnp.arange(S)
    cos_a, sin_a = rope_tables(t, ROPE_DIMS_A, ROPE_THETA)
    qa = partial_rope(qa.reshape(B, S, N_HEADS_A, HEAD_DIM), cos_a, sin_a)
    ka = partial_rope(ka.reshape(B, S, N_HEADS_A, HEAD_DIM), cos_a, sin_a)
    va = va.reshape(B, S, N_HEADS_A, HEAD_DIM)
    o_a = mixer_dilated(qa, ka, va)
    rows = S // GRID_W
    row_pos = jnp.repeat(jnp.arange(rows), GRID_W)
    col_pos = jnp.tile(jnp.arange(GRID_W), rows)
    cos_r, sin_r = rope_tables(row_pos, HEAD_DIM // 2, AXIAL_THETA)
    cos_c, sin_c = rope_tables(col_pos, HEAD_DIM // 2, AXIAL_THETA)
    qb = axial_rope(rmsnorm(qb.reshape(B, S, N_HEADS_B, HEAD_DIM), qn_g), cos_r, sin_r, cos_c, sin_c)
    kb = axial_rope(rmsnorm(kb.reshape(B, S, N_KV_B, HEAD_DIM), kn_g), cos_r, sin_r, cos_c, sin_c)
    vb = vb.reshape(B, S, N_KV_B, HEAD_DIM)
    o_b = mixer_axial_gqa(qb, kb, vb)
    o = jnp.concatenate([rmsnorm(o_a, on_a_g), rmsnorm(o_b, on_b_g)], axis=-1)
    x = x + g1 * jnp.dot(o, w_out)
    h2 = rmsnorm(x, ln2_g) * (1 + sc2) + sh2
    x = x + g2 * hier_moe(h2, w_rg, b_rg, w_re, b_re, w1, w3, w2)
    return x


def trunk(x, c, ln1_g, ln2_g, w_ada, b_ada, w_in, w_out, qn_g, kn_g, on_a_g, on_b_g,
          w_rg, b_rg, w_re, b_re, w1, w3, w2, lnf_g, w_adaf, b_adaf):
    for l in range(DEPTH):
        x = encoder_layer(x, c, ln1_g[l], ln2_g[l], w_ada[l], b_ada[l], w_in[l], w_out[l],
                          qn_g[l], kn_g[l], on_a_g[l], on_b_g[l], w_rg[l], b_rg[l], w_re[l], b_re[l],
                          w1[l], w3[l], w2[l])
    modf = (jnp.dot(jax.nn.silu(c), w_adaf, preferred_element_type=F32) + b_adaf.astype(F32)).astype(x.dtype)[:, None, :]
    shf, scf = jnp.split(modf, 2, axis=-1)
    return rmsnorm(x, lnf_g) * (1 + scf) + shf


def setup_inputs(seed: int = 0) -> dict:
    key = jax.random.key(seed)
    ks = jax.random.split(key, 26)
    D = D_MODEL
    nrm = jax.random.normal
    return {
        'x_prompt': nrm(ks[0], (BATCH, SEQ, D), F32),
        'x_sample': nrm(ks[1], (DEC_BATCH, DEC_SEQ, D), F32),
        'c_prompt': nrm(ks[2], (BATCH, D), F32),
        'c_sample': nrm(ks[3], (DEC_BATCH, D), F32),
        'ln1_g': 1.0 + 0.05 * nrm(ks[4], (DEPTH, D), F32),
        'ln2_g': 1.0 + 0.05 * nrm(ks[5], (DEPTH, D), F32),
        'w_ada': 0.5 * D ** -0.5 * nrm(ks[6], (DEPTH, D, 6 * D), F32),
        'b_ada': 0.02 * nrm(ks[7], (DEPTH, 6 * D), F32),
        'w_in': D ** -0.5 * nrm(ks[8], (DEPTH, D, IN_COLS), F32),
        'w_out': MIX_WIDTH ** -0.5 * nrm(ks[9], (DEPTH, MIX_WIDTH, D), F32),
        'qn_g': 1.0 + 0.05 * nrm(ks[10], (DEPTH, HEAD_DIM), F32),
        'kn_g': 1.0 + 0.05 * nrm(ks[11], (DEPTH, HEAD_DIM), F32),
        'on_a_g': 1.0 + 0.05 * nrm(ks[12], (DEPTH, WIDTH_A), F32),
        'on_b_g': 1.0 + 0.05 * nrm(ks[13], (DEPTH, WIDTH_B), F32),
        'w_rg': D ** -0.5 * nrm(ks[14], (DEPTH, D, N_GROUPS), F32),
        'b_rg': 0.01 * nrm(ks[15], (DEPTH, N_GROUPS), F32),
        'w_re': D ** -0.5 * nrm(ks[16], (DEPTH, D, N_EXPERTS), F32),
        'b_re': 0.01 * nrm(ks[17], (DEPTH, N_EXPERTS), F32),
        'w1': D ** -0.5 * nrm(ks[18], (DEPTH, N_EXPERTS, D, D_EXPERT), F32),
        'w3': D ** -0.5 * nrm(ks[19], (DEPTH, N_EXPERTS, D, D_EXPERT), F32),
        'w2': D_EXPERT ** -0.5 * nrm(ks[20], (DEPTH, N_EXPERTS, D_EXPERT, D), F32),
        'lnf_g': 1.0 + 0.05 * nrm(ks[21], (D,), F32),
        'w_adaf': 0.5 * D ** -0.5 * nrm(ks[22], (D, 2 * D), F32),
        'b_adaf': 0.02 * nrm(ks[23], (2 * D,), F32),
    }


def reference(x_prompt, x_sample, c_prompt, c_sample, ln1_g, ln2_g, w_ada, b_ada, w_in, w_out,
              qn_g, kn_g, on_a_g, on_b_g, w_rg, b_rg, w_re, b_re, w1, w3, w2, lnf_g, w_adaf, b_adaf):
    y_prompt = trunk(x_prompt, c_prompt, ln1_g, ln2_g, w_ada, b_ada, w_in, w_out, qn_g, kn_g, on_a_g, on_b_g,
                     w_rg, b_rg, w_re, b_re, w1, w3, w2, lnf_g, w_adaf, b_adaf)
    y_sample = trunk(x_sample, c_sample, ln1_g, ln2_g, w_ada, b_ada, w_in, w_out, qn_g, kn_g, on_a_g, on_b_g,
                     w_rg, b_rg, w_re, b_re, w1, w3, w2, lnf_g, w_adaf, b_adaf)
    return (y_prompt, y_sample)
```

```python
import functools

import jax
import jax.numpy as jnp
from jax import lax
from jax.experimental import pallas as pl
from jax.experimental.pallas import tpu as pltpu

F32 = jnp.float32
BF16 = jnp.bfloat16
I32 = jnp.int32

D_MODEL = 1024
HEAD_DIM = 64
N_HEADS = 8
WIDTH = N_HEADS * HEAD_DIM
KV_WIDTH = 2 * HEAD_DIM
IN_COLS = 3 * WIDTH + WIDTH + 2 * KV_WIDTH
ROPE_THETA = 500000.0
ROPE_DIMS_A = 16
AXIAL_THETA = 10000.0
GRID_W = 64
N_GROUPS = 4
EXPERTS_PER_GROUP = 8
N_EXPERTS = 32
D_EXPERT = 512
EPS = 1e-6
NEG = -1e30
DIL_W = 64
DILATIONS = (1, 4, 16)

LANES = 128
VMEM_LIMIT = 56 * 1024 * 1024

TM_PROJ = 512
TILE_A = 1024
TQ_B = 128
TK_B = 1024
BM_MOE = 256
TM_FIN = 512


def _cparams(sem):
    return pltpu.CompilerParams(dimension_semantics=sem, vmem_limit_bytes=VMEM_LIMIT)


def _mod_kernel(c_ref, w_ref, b_ref, o_ref):
    c = c_ref[...]
    a = c / (1.0 + jnp.exp(-c))
    o_ref[...] = jnp.dot(a, w_ref[...], preferred_element_type=F32,
                         precision=lax.Precision.HIGHEST) + b_ref[...]


def _modulation(c_pad, w, b):
    rows, d = c_pad.shape
    n = w.shape[1]
    tn = 1024
    return pl.pallas_call(
        _mod_kernel,
        grid=(n // tn,),
        in_specs=[pl.BlockSpec((rows, d), lambda j: (0, 0)),
                  pl.BlockSpec((d, tn), lambda j: (0, j)),
                  pl.BlockSpec((1, tn), lambda j: (0, j))],
        out_specs=pl.BlockSpec((rows, tn), lambda j: (0, j)),
        out_shape=jax.ShapeDtypeStruct((rows, n), F32),
        compiler_params=_cparams(("arbitrary",)),
        name="modulation",
    )(c_pad, w, b.reshape(1, n))


def _rope_tables(seq):
    t = jnp.arange(seq)
    tf = t.astype(F32)
    inv_a = ROPE_THETA ** (-jnp.arange(0, ROPE_DIMS_A, 2, dtype=F32) / ROPE_DIMS_A)
    ang = tf[:, None] * inv_a[None, :]
    cos, sin = jnp.cos(ang), jnp.sin(ang)
    rest = HEAD_DIM - ROPE_DIMS_A
    one = jnp.ones((seq, rest), F32)
    zero = jnp.zeros((seq, rest), F32)
    z8 = jnp.zeros_like(sin)
    cos64 = jnp.concatenate([cos, cos, one], 1)
    sm64 = jnp.concatenate([-sin, z8, zero], 1)
    sp64 = jnp.concatenate([z8, sin, zero], 1)
    ta = jnp.stack([jnp.tile(a, (1, 2)) for a in (cos64, sm64, sp64)])

    n_ax = HEAD_DIM // 2
    inv_b = AXIAL_THETA ** (-jnp.arange(0, n_ax, 2, dtype=F32) / n_ax)
    row_pos = (t // GRID_W).astype(F32)
    col_pos = (t % GRID_W).astype(F32)
    ar = row_pos[:, None] * inv_b[None, :]
    ac = col_pos[:, None] * inv_b[None, :]
    z16 = jnp.zeros_like(ar)
    cos64 = jnp.concatenate([jnp.cos(ar), jnp.cos(ar), jnp.cos(ac), jnp.cos(ac)], 1)
    sm64 = jnp.concatenate([-jnp.sin(ar), z16, -jnp.sin(ac), z16], 1)
    sp64 = jnp.concatenate([z16, jnp.sin(ar), z16, jnp.sin(ac)], 1)
    tb = jnp.stack([jnp.tile(a, (1, 2)) for a in (cos64, sm64, sp64)])
    return ta, tb


def _inproj_kernel(x_ref, mod_ref, g_ref, w_ref, ta_ref, tb_ref, qg_ref, kg_ref, bd_ref,
                   qa_ref, ka_ref, va_ref, qb_ref, kb_ref, vb_ref):
    x = x_ref[...]
    mod = mod_ref[0]
    inv = lax.rsqrt(jnp.mean(x * x, axis=-1, keepdims=True) + EPS)
    h = (x * inv) * (g_ref[...] * (1.0 + mod[1:2])) + mod[0:1]
    hb = h.astype(BF16)
    cos_a, sm_a, sp_a = ta_ref[0], ta_ref[1], ta_ref[2]
    cos_b, sm_b, sp_b = tb_ref[0], tb_ref[1], tb_ref[2]
    bd = bd_ref[...]
    lane = lax.broadcasted_iota(I32, (1, LANES), 1)
    low = lane < HEAD_DIM

    def proj(lo, n):
        return jnp.dot(hb, w_ref[:, lo:lo + n], preferred_element_type=F32)

    def rope(y, cos, sm, sp, sh):
        return y * cos + pltpu.roll(y, LANES - sh, 1) * sm + pltpu.roll(y, sh, 1) * sp

    def headnorm(y, g):
        t = y * y
        thi = t.astype(BF16)
        tlo = (t - thi.astype(F32)).astype(BF16)
        ss = jnp.dot(jnp.concatenate([thi, tlo], axis=1), bd, preferred_element_type=F32)
        return y * lax.rsqrt(ss * (1.0 / HEAD_DIM) + EPS) * g

    scale = HEAD_DIM ** -0.5
    qa = proj(0, WIDTH)
    ka = proj(WIDTH, WIDTH)
    for c in range(WIDTH // LANES):
        sl = slice(c * LANES, (c + 1) * LANES)
        qa_ref[:, sl] = rope(qa[:, sl], cos_a, sm_a, sp_a, 8) * scale
        ka_ref[:, sl] = rope(ka[:, sl], cos_a, sm_a, sp_a, 8)
    va_ref[...] = proj(2 * WIDTH, WIDTH)

    qb = proj(3 * WIDTH, WIDTH)
    qg = qg_ref[...]
    for c in range(WIDTH // LANES):
        y = rope(headnorm(qb[:, c * LANES:(c + 1) * LANES], qg), cos_b, sm_b, sp_b, 16) * scale
        ysw = pltpu.roll(y, HEAD_DIM, 1)
        grp = c // 2
        h0 = y if grp == 0 else ysw
        h1 = ysw if grp == 0 else y
        keep = low if grp == 0 else jnp.logical_not(low)
        qb_ref[2 * c] = jnp.where(keep, h0, 0.0).astype(BF16)
        qb_ref[2 * c + 1] = jnp.where(keep, h1, 0.0).astype(BF16)
    kb = proj(4 * WIDTH, KV_WIDTH)
    kb_ref[...] = rope(headnorm(kb, kg_ref[...]), cos_b, sm_b, sp_b, 16).astype(BF16)
    vb_ref[...] = proj(4 * WIDTH + KV_WIDTH, KV_WIDTH).astype(BF16)


def _inproj(x2d, mod, boff, seq, ln1_g, w_in_bf, ta, tb, qg, kg, bd):
    t_tok = x2d.shape[0]
    tm = TM_PROJ
    per_seq = seq // tm
    tok = lambda i: (i, 0)
    const = lambda i: (0, 0)
    return pl.pallas_call(
        _inproj_kernel,
        grid=(t_tok // tm,),
        in_specs=[
            pl.BlockSpec((tm, D_MODEL), tok),
            pl.BlockSpec((1, 6, D_MODEL), lambda i: (boff + i // per_seq, 0, 0)),
            pl.BlockSpec((1, D_MODEL), const),
            pl.BlockSpec((D_MODEL, IN_COLS), const),
            pl.BlockSpec((3, tm, LANES), lambda i: (0, i % per_seq, 0)),
            pl.BlockSpec((3, tm, LANES), lambda i: (0, i % per_seq, 0)),
            pl.BlockSpec((1, LANES), const),
            pl.BlockSpec((1, LANES), const),
            pl.BlockSpec((2 * LANES, LANES), const),
        ],
        out_specs=[
            pl.BlockSpec((tm, WIDTH), tok),
            pl.BlockSpec((tm, WIDTH), tok),
            pl.BlockSpec((tm, WIDTH), tok),
            pl.BlockSpec((N_HEADS, tm, LANES), lambda i: (0, i, 0)),
            pl.BlockSpec((tm, KV_WIDTH), tok),
            pl.BlockSpec((tm, KV_WIDTH), tok),
        ],
        out_shape=[
            jax.ShapeDtypeStruct((t_tok, WIDTH), F32),
            jax.ShapeDtypeStruct((t_tok, WIDTH), F32),
            jax.ShapeDtypeStruct((t_tok, WIDTH), F32),
            jax.ShapeDtypeStruct((N_HEADS, t_tok, LANES), BF16),
            jax.ShapeDtypeStruct((t_tok, KV_WIDTH), BF16),
            jax.ShapeDtypeStruct((t_tok, KV_WIDTH), BF16),
        ],
        compiler_params=_cparams(("parallel",)),
        name="inproj",
    )(x2d, mod, ln1_g.reshape(1, D_MODEL), w_in_bf, ta, tb, qg, kg, bd)


def _attn_a_kernel(q_ref, kp_ref, kc_ref, kn_ref, vp_ref, vc_ref, vn_ref, o_ref,
                   kwin, vwin, m1, l1, a1, m2, l2, a2, m3, l3, a3, *, seq):
    n = pl.program_id(1)
    t = TILE_A
    w = DIL_W
    kwin[0:t] = kp_ref[...]
    kwin[t:2 * t] = kc_ref[...]
    kwin[2 * t:3 * t] = kn_ref[...]
    vwin[0:t] = vp_ref[...]
    vwin[t:2 * t] = vc_ref[...]
    vwin[2 * t:3 * t] = vn_ref[...]

    lane = lax.broadcasted_iota(I32, (1, LANES), 1)
    low = lane < HEAD_DIM
    row_i = lax.broadcasted_iota(I32, (2 * w, 1), 0) & (w - 1)
    col_m = lax.broadcasted_iota(I32, (1, 3 * w), 1)
    band = jnp.abs(col_m - w - row_i) <= w
    base = n * t - t

    for d, m_s, l_s, a_s in ((1, m1, l1, a1), (4, m2, l2, a2), (16, m3, l3, a3)):
        n_sub = t // (w * d)

        def body(j, carry, d=d, m_s=m_s, l_s=l_s, a_s=a_s, n_sub=n_sub):
            qs = (j // d) * (w * d) + (j % d) if d > 1 else j * w
            ks = qs + t - w * d
            q = q_ref[pl.ds(qs, w, stride=d), :]
            k = kwin[pl.ds(ks, 3 * w, stride=d), :].astype(BF16)
            v = vwin[pl.ds(ks, 3 * w, stride=d), :].astype(BF16)
            q2 = jnp.concatenate([jnp.where(low, q, 0.0), jnp.where(low, 0.0, q)], axis=0).astype(BF16)
            s = lax.dot_general(q2, k, (((1,), (1,)), ((), ())), preferred_element_type=F32)
            kpos = base + ks + d * col_m
            valid = band & (kpos >= 0) & (kpos < seq)
            s = jnp.where(valid, s, NEG)
            mx = jnp.max(s, axis=-1, keepdims=True)
            p = jnp.exp(s - mx)
            den = jnp.sum(p, axis=-1, keepdims=True)
            o = jnp.dot(p.astype(BF16), v, preferred_element_type=F32)
            m_s[pl.ds(qs, w, stride=d), :] = jnp.where(low, mx[0:w], mx[w:2 * w])
            l_s[pl.ds(qs, w, stride=d), :] = jnp.where(low, den[0:w], den[w:2 * w])
            a_s[pl.ds(qs, w, stride=d), :] = jnp.where(low, o[0:w], o[w:2 * w])
            return carry

        lax.fori_loop(0, t // w, body, 0)

    mm = jnp.maximum(jnp.maximum(m1[...], m2[...]), m3[...])
    w1 = jnp.exp(m1[...] - mm)
    w2 = jnp.exp(m2[...] - mm)
    w3 = jnp.exp(m3[...] - mm)
    num = w1 * a1[...] + w2 * a2[...] + w3 * a3[...]
    den = w1 * l1[...] + w2 * l2[...] + w3 * l3[...]
    o_ref[...] = (num / den).astype(o_ref.dtype)


def _attn_a(qa, ka, va, batch, seq):
    t = TILE_A
    nt = seq // t
    q3 = qa.reshape(batch, seq, WIDTH)
    k3 = ka.reshape(batch, seq, WIDTH)
    v3 = va.reshape(batch, seq, WIDTH)
    cur = lambda b, n, p: (b, n, p)
    prev = lambda b, n, p: (b, jnp.maximum(n - 1, 0), p)
    nxt = lambda b, n, p: (b, jnp.minimum(n + 1, nt - 1), p)
    blk = (None, t, LANES)
    scr = pltpu.VMEM((t, LANES), F32)
    out = pl.pallas_call(
        functools.partial(_attn_a_kernel, seq=seq),
        grid=(batch, nt, WIDTH // LANES),
        in_specs=[pl.BlockSpec(blk, cur),
                  pl.BlockSpec(blk, prev), pl.BlockSpec(blk, cur), pl.BlockSpec(blk, nxt),
                  pl.BlockSpec(blk, prev), pl.BlockSpec(blk, cur), pl.BlockSpec(blk, nxt)],
        out_specs=pl.BlockSpec(blk, cur),
        out_shape=jax.ShapeDtypeStruct((batch, seq, WIDTH), BF16),
        scratch_shapes=[pltpu.VMEM((3 * t, LANES), F32), pltpu.VMEM((3 * t, LANES), F32)] + [scr] * 9,
        compiler_params=_cparams(("parallel", "parallel", "parallel")),
        name="attn_dilated",
    )(q3, k3, k3, k3, v3, v3, v3)
    return out.reshape(batch * seq, WIDTH)


def _attn_b_kernel(q_ref, k_ref, v_ref, o_ref, m_s, l_s, acc, *, tq):
    kv = pl.program_id(2)

    @pl.when(kv == 0)
    def _():
        m_s[...] = jnp.full(m_s.shape, NEG, F32)
        l_s[...] = jnp.zeros(l_s.shape, F32)
        acc[...] = jnp.zeros(acc.shape, F32)

    q8 = q_ref[...].reshape(N_HEADS * tq, LANES)
    s = lax.dot_general(q8, k_ref[...], (((1,), (1,)), ((), ())), preferred_element_type=F32)
    m_prev = m_s[...]
    m_new = jnp.maximum(m_prev, jnp.max(s, axis=-1, keepdims=True))
    alpha = jnp.exp(m_prev - m_new)
    p = jnp.exp(s - m_new)
    l_s[...] = alpha * l_s[...] + jnp.sum(p, axis=-1, keepdims=True)
    acc[...] = alpha * acc[...] + jnp.dot(p.astype(BF16), v_ref[...], preferred_element_type=F32)
    m_s[...] = m_new

    @pl.when(kv == pl.num_programs(2) - 1)
    def _():
        lane = lax.broadcasted_iota(I32, (1, LANES), 1)
        low = lane < HEAD_DIM
        o = acc[...] / l_s[...]
        for c in range(N_HEADS // 2):
            h0 = o[(2 * c) * tq:(2 * c + 1) * tq]
            h1 = o[(2 * c + 1) * tq:(2 * c + 2) * tq]
            if c // 2 == 0:
                chunk = jnp.where(low, h0, pltpu.roll(h1, HEAD_DIM, 1))
            else:
                chunk = jnp.where(low, pltpu.roll(h0, HEAD_DIM, 1), h1)
            o_ref[:, c * LANES:(c + 1) * LANES] = chunk.astype(o_ref.dtype)


def _attn_b(qb8, kb, vb, batch, seq):
    tq, tk = TQ_B, min(TK_B, seq)
    nq = seq // tq
    nk = seq // tk
    return pl.pallas_call(
        functools.partial(_attn_b_kernel, tq=tq),
        grid=(batch, nq, nk),
        in_specs=[pl.BlockSpec((N_HEADS, tq, LANES), lambda b, i, j: (0, b * nq + i, 0)),
                  pl.BlockSpec((tk, KV_WIDTH), lambda b, i, j: (b * nk + j, 0)),
                  pl.BlockSpec((tk, KV_WIDTH), lambda b, i, j: (b * nk + j, 0))],
        out_specs=pl.BlockSpec((tq, WIDTH), lambda b, i, j: (b * nq + i, 0)),
        out_shape=jax.ShapeDtypeStruct((batch * seq, WIDTH), BF16),
        scratch_shapes=[pltpu.VMEM((N_HEADS * tq, 1), F32), pltpu.VMEM((N_HEADS * tq, 1), F32),
                        pltpu.VMEM((N_HEADS * tq, LANES), F32)],
        compiler_params=_cparams(("parallel", "parallel", "arbitrary")),
        name="attn_dense",
    )(qb8, kb, vb)


def _outproj_kernel(x_ref, oa_ref, ob_ref, mod_ref, ga_ref, gb_ref, wo_ref, ln2_ref,
                    wrh_ref, wrl_ref, br_ref, x1_ref, h2_ref, rt_ref):
    def rms(o, g):
        return o * lax.rsqrt(jnp.mean(o * o, axis=-1, keepdims=True) + EPS) * g

    oa = rms(oa_ref[...].astype(F32), ga_ref[...])
    ob = rms(ob_ref[...].astype(F32), gb_ref[...])
    o = jnp.concatenate([oa, ob], axis=1).astype(BF16)
    mod = mod_ref[0]
    x1 = x_ref[...] + mod[2:3] * jnp.dot(o, wo_ref[...], preferred_element_type=F32)
    x1_ref[...] = x1
    inv = lax.rsqrt(jnp.mean(x1 * x1, axis=-1, keepdims=True) + EPS)
    h2 = (x1 * inv) * (ln2_ref[...] * (1.0 + mod[4:5])) + mod[3:4]
    h2_ref[...] = h2

    hh = h2.astype(BF16)
    hl = (h2 - hh.astype(F32)).astype(BF16)
    wrh = wrh_ref[...]
    logits = (jnp.dot(hh, wrh, preferred_element_type=F32)
              + jnp.dot(hl, wrh, preferred_element_type=F32)
              + jnp.dot(hh, wrl_ref[...], preferred_element_type=F32)) + br_ref[...]

    lane = lax.broadcasted_iota(I32, logits.shape, 1).astype(F32)
    big = jnp.float32(LANES)
    is_g = lane < N_GROUPS
    gl = jnp.where(is_g, logits, -jnp.inf)
    gmax = jnp.max(gl, axis=-1, keepdims=True)
    g_sel = jnp.min(jnp.where(gl == gmax, lane, big), axis=-1, keepdims=True)
    g_den = jnp.sum(jnp.exp(gl - gmax), axis=-1, keepdims=True)
    g_w = 1.0 / g_den
    e_lo = N_GROUPS + EXPERTS_PER_GROUP * g_sel
    is_e = (lane >= e_lo) & (lane < e_lo + EXPERTS_PER_GROUP)
    el = jnp.where(is_e, logits, -jnp.inf)
    emax = jnp.max(el, axis=-1, keepdims=True)
    p = jnp.where(is_e, jnp.exp(el - emax), -1.0)
    p1 = jnp.max(p, axis=-1, keepdims=True)
    i1 = jnp.min(jnp.where(p == p1, lane, big), axis=-1, keepdims=True)
    pm = jnp.where(lane == i1, -1.0, p)
    p2 = jnp.max(pm, axis=-1, keepdims=True)
    i2 = jnp.min(jnp.where(pm == p2, lane, big), axis=-1, keepdims=True)
    tot = p1 + p2
    gate1 = g_w * p1 / tot
    gate2 = g_w * p2 / tot
    e1 = i1 - N_GROUPS
    e2 = i2 - N_GROUPS
    rt_ref[...] = jnp.where(lane == 0, gate1, jnp.where(lane == 1, gate2,
                            jnp.where(lane == 2, e1, jnp.where(lane == 3, e2, 0.0))))


def _outproj(x2d, oa, ob, mod, boff, seq, ga, gb, wo_bf, ln2_g, wrh, wrl, br):
    t_tok = x2d.shape[0]
    tm = TM_PROJ
    per_seq = seq // tm
    tok = lambda i: (i, 0)
    const = lambda i: (0, 0)
    return pl.pallas_call(
        _outproj_kernel,
        grid=(t_tok // tm,),
        in_specs=[
            pl.BlockSpec((tm, D_MODEL), tok),
            pl.BlockSpec((tm, WIDTH), tok),
            pl.BlockSpec((tm, WIDTH), tok),
            pl.BlockSpec((1, 6, D_MODEL), lambda i: (boff + i // per_seq, 0, 0)),
            pl.BlockSpec((1, WIDTH), const),
            pl.BlockSpec((1, WIDTH), const),
            pl.BlockSpec((D_MODEL, D_MODEL), const),
            pl.BlockSpec((1, D_MODEL), const),
            pl.BlockSpec((D_MODEL, LANES), const),
            pl.BlockSpec((D_MODEL, LANES), const),
            pl.BlockSpec((1, LANES), const),
        ],
        out_specs=[pl.BlockSpec((tm, D_MODEL), tok),
                   pl.BlockSpec((tm, D_MODEL), tok),
                   pl.BlockSpec((tm, LANES), tok)],
        out_shape=[jax.ShapeDtypeStruct((t_tok, D_MODEL), F32),
                   jax.ShapeDtypeStruct((t_tok, D_MODEL), F32),
                   jax.ShapeDtypeStruct((t_tok, LANES), F32)],
        compiler_params=_cparams(("parallel",)),
        name="outproj_router",
    )(x2d, oa, ob, mod, ga.reshape(1, WIDTH), gb.reshape(1, WIDTH), wo_bf,
      ln2_g.reshape(1, D_MODEL), wrh, wrl, br)


def _moe_kernel(be_ref, tokc_ref, tokn_ref, row_ref, h2_hbm, w1_ref, w3_ref, w2_ref, y_hbm,
                xbuf, ybuf, gsem, ssem):
    del be_ref
    i = pl.program_id(0)
    n = pl.num_programs(0)
    bm = BM_MOE
    slot = i % 2

    def start_gather(tok_ref, dst_slot):
        def body(r, carry):
            tkn = tok_ref[0, 0, r]
            pltpu.make_async_copy(h2_hbm.at[pl.ds(tkn, 1)], xbuf.at[dst_slot, pl.ds(r, 1)],
                                  gsem.at[dst_slot]).start()
            return carry
        lax.fori_loop(0, bm, body, 0)

    def wait_rows(hbm, buf, sem, s):
        def body(r, carry):
            pltpu.make_async_copy(hbm.at[pl.ds(0, 1)], buf.at[s, pl.ds(0, 1)], sem.at[s]).wait()
            return carry
        lax.fori_loop(0, bm, body, 0)

    @pl.when(i == 0)
    def _():
        start_gather(tokc_ref, 0)

    @pl.when(i + 1 < n)
    def _():
        start_gather(tokn_ref, 1 - slot)

    wait_rows(h2_hbm, xbuf, gsem, slot)
    x = xbuf[slot].astype(BF16)
    a = jnp.dot(x, w1_ref[...], preferred_element_type=F32)
    b = jnp.dot(x, w3_ref[...], preferred_element_type=F32)
    hmid = ((a / (1.0 + jnp.exp(-a))) * b).astype(BF16)
    y = jnp.dot(hmid, w2_ref[...], preferred_element_type=F32)

    @pl.when(i >= 2)
    def _():
        wait_rows(y_hbm, ybuf, ssem, slot)

    ybuf[slot] = y

    def scatter(r, carry):
        dst = row_ref[0, 0, r]
        pltpu.make_async_copy(ybuf.at[slot, pl.ds(r, 1)], y_hbm.at[pl.ds(dst, 1)], ssem.at[slot]).start()
        return carry
    lax.fori_loop(0, bm, scatter, 0)

    @pl.when(i == n - 1)
    def _():
        wait_rows(y_hbm, ybuf, ssem, slot)
        wait_rows(y_hbm, ybuf, ssem, 1 - slot)


def _moe(h2, block_e, slot_tok, out_row, w1_bf, w3_bf, w2_bf):
    bm = BM_MOE
    n_blocks = block_e.shape[0]
    p_rows = n_blocks * bm
    tok3 = slot_tok.reshape(n_blocks, 1, bm)
    row3 = out_row.reshape(n_blocks, 1, bm)
    smem_blk = lambda fn: pl.BlockSpec((1, 1, bm), fn, memory_space=pltpu.SMEM)
    grid_spec = pltpu.PrefetchScalarGridSpec(
        num_scalar_prefetch=1,
        grid=(n_blocks,),
        in_specs=[
            smem_blk(lambda i, be: (i, 0, 0)),
            smem_blk(lambda i, be: (jnp.minimum(i + 1, n_blocks - 1), 0, 0)),
            smem_blk(lambda i, be: (i, 0, 0)),
            pl.BlockSpec(memory_space=pl.ANY),
            pl.BlockSpec((None, D_MODEL, D_EXPERT), lambda i, be: (be[i], 0, 0)),
            pl.BlockSpec((None, D_MODEL, D_EXPERT), lambda i, be: (be[i], 0, 0)),
            pl.BlockSpec((None, D_EXPERT, D_MODEL), lambda i, be: (be[i], 0, 0)),
        ],
        out_specs=pl.BlockSpec(memory_space=pl.ANY),
        scratch_shapes=[pltpu.VMEM((2, bm, D_MODEL), F32), pltpu.VMEM((2, bm, D_MODEL), F32),
                        pltpu.SemaphoreType.DMA((2,)), pltpu.SemaphoreType.DMA((2,))],
    )
    return pl.pallas_call(
        _moe_kernel,
        grid_spec=grid_spec,
        out_shape=jax.ShapeDtypeStruct((p_rows, D_MODEL), F32),
        compiler_params=_cparams(("arbitrary",)),
        name="moe_experts",
    )(block_e, tok3, tok3, row3, h2, w1_bf, w3_bf, w2_bf)


def _moe_plan(expert, bm):
    t_tok = expert.shape[0]
    a_tot = 2 * t_tok
    flat_e = expert.reshape(-1)
    onehot = (flat_e[:, None] == jnp.arange(N_EXPERTS, dtype=I32)[None, :]).astype(I32)
    csum = jnp.cumsum(onehot, axis=0)
    rank = jnp.take_along_axis(csum, flat_e[:, None], axis=1)[:, 0] - 1
    counts = csum[-1]
    pcounts = (counts + bm - 1) // bm * bm
    pends = jnp.cumsum(pcounts)
    pstarts = pends - pcounts
    dest = pstarts[flat_e] + rank
    n_blocks = a_tot // bm + N_EXPERTS
    p_rows = n_blocks * bm
    slot_assign = jnp.full((p_rows,), -1, I32).at[dest].set(jnp.arange(a_tot, dtype=I32))
    is_pad = slot_assign < 0
    pad_rank = jnp.cumsum(is_pad.astype(I32)) - 1
    out_row = jnp.where(is_pad, a_tot + pad_rank, slot_assign).astype(I32)
    slot_tok = jnp.where(is_pad, 0, slot_assign // 2).astype(I32)
    block_e = jnp.minimum(
        jnp.searchsorted(pends, jnp.arange(n_blocks, dtype=I32) * bm, side="right"), N_EXPERTS - 1).astype(I32)
    return block_e, slot_tok, out_row


def _final_kernel(x1_ref, y_ref, rt_ref, mod_ref, modf_ref, g_ref, o_ref):
    rt = rt_ref[...]
    moe = rt[:, 0:1] * y_ref[:, 0:D_MODEL] + rt[:, 1:2] * y_ref[:, D_MODEL:2 * D_MODEL]
    x2 = x1_ref[...] + mod_ref[0][5:6] * moe
    modf = modf_ref[0]
    inv = lax.rsqrt(jnp.mean(x2 * x2, axis=-1, keepdims=True) + EPS)
    o_ref[...] = (x2 * inv) * (g_ref[...] * (1.0 + modf[1:2])) + modf[0:1]


def _final(x1, y2, rt, mod, modf, boff, seq, lnf_g):
    t_tok = x1.shape[0]
    tm = TM_FIN
    per_seq = seq // tm
    tok = lambda i: (i, 0)
    ypair = y2.reshape(y2.shape[0] // 2, 2 * D_MODEL)
    return pl.pallas_call(
        _final_kernel,
        grid=(t_tok // tm,),
        in_specs=[pl.BlockSpec((tm, D_MODEL), tok),
                  pl.BlockSpec((tm, 2 * D_MODEL), tok),
                  pl.BlockSpec((tm, LANES), tok),
                  pl.BlockSpec((1, 6, D_MODEL), lambda i: (boff + i // per_seq, 0, 0)),
                  pl.BlockSpec((1, 2, D_MODEL), lambda i: (boff + i // per_seq, 0, 0)),
                  pl.BlockSpec((1, D_MODEL), lambda i: (0, 0))],
        out_specs=pl.BlockSpec((tm, D_MODEL), tok),
        out_shape=jax.ShapeDtypeStruct((t_tok, D_MODEL), F32),
        compiler_params=_cparams(("parallel",)),
        name="combine_final",
    )(x1, ypair, rt, mod, modf, lnf_g.reshape(1, D_MODEL))


def _trunk(x, boff, mod, modf, prm):
    batch, seq, _ = x.shape
    x2d = x.reshape(batch * seq, D_MODEL)
    ta, tb = _rope_tables(seq)
    qa, ka, va, qb8, kb, vb = _inproj(x2d, mod, boff, seq, prm["ln1_g"], prm["w_in"], ta, tb,
                                      prm["qg"], prm["kg"], prm["bd"])
    oa = _attn_a(qa, ka, va, batch, seq)
    ob = _attn_b(qb8, kb, vb, batch, seq)
    x1, h2, rt = _outproj(x2d, oa, ob, mod, boff, seq, prm["on_a_g"], prm["on_b_g"], prm["w_out"],
                          prm["ln2_g"], prm["wrh"], prm["wrl"], prm["br"])
    expert = rt[:, 2:4].astype(I32)
    block_e, slot_tok, out_row = _moe_plan(expert, BM_MOE)
    y2 = _moe(h2, block_e, slot_tok, out_row, prm["w1"], prm["w3"], prm["w2"])
    y = _final(x1, y2, rt, mod, modf, boff, seq, prm["lnf_g"])
    return y.reshape(batch, seq, D_MODEL)


def kernel(x_prompt, x_sample, c_prompt, c_sample, ln1_g, ln2_g, w_ada, b_ada, w_in, w_out, qn_g, kn_g,
           on_a_g, on_b_g, w_rg, b_rg, w_re, b_re, w1, w3, w2, lnf_g, w_adaf, b_adaf):
    assert ln1_g.shape[0] == 1, "single-layer trunk"
    nb_p, nb_s = c_prompt.shape[0], c_sample.shape[0]
    rows = -(-(nb_p + nb_s) // 8) * 8
    c_all = jnp.concatenate([c_prompt, c_sample, jnp.zeros((rows - nb_p - nb_s, D_MODEL), F32)], axis=0)
    mod = _modulation(c_all, w_ada[0], b_ada[0]).reshape(rows, 6, D_MODEL)
    modf = _modulation(c_all, w_adaf, b_adaf).reshape(rows, 2, D_MODEL)

    wr = jnp.concatenate([w_rg[0], w_re[0], jnp.zeros((D_MODEL, LANES - N_GROUPS - N_EXPERTS), F32)], axis=1)
    wrh = wr.astype(BF16)
    wrl = (wr - wrh.astype(F32)).astype(BF16)
    br = jnp.concatenate([b_rg[0], b_re[0], jnp.zeros((LANES - N_GROUPS - N_EXPERTS,), F32)]).reshape(1, LANES)
    seg = lax.broadcasted_iota(I32, (2 * LANES, LANES), 0) % LANES // HEAD_DIM
    bd = (seg == lax.broadcasted_iota(I32, (2 * LANES, LANES), 1) // HEAD_DIM).astype(BF16)
    prm = dict(
        ln1_g=ln1_g[0], ln2_g=ln2_g[0], w_in=w_in[0].astype(BF16), w_out=w_out[0].astype(BF16),
        qg=jnp.tile(qn_g[0], 2).reshape(1, LANES), kg=jnp.tile(kn_g[0], 2).reshape(1, LANES), bd=bd,
        on_a_g=on_a_g[0], on_b_g=on_b_g[0], wrh=wrh, wrl=wrl, br=br,
        w1=w1[0].astype(BF16), w3=w3[0].astype(BF16), w2=w2[0].astype(BF16), lnf_g=lnf_g,
    )
    y_prompt = _trunk(x_prompt, 0, mod, modf, prm)
    y_sample = _trunk(x_sample, nb_p, mod, modf, prm)
    return (y_prompt, y_sample)
```

```python
import functools

import jax
import jax.numpy as jnp
from jax import lax
from jax.experimental import pallas as pl
from jax.experimental.pallas import tpu as pltpu

F32 = jnp.float32
BF16 = jnp.bfloat16
I32 = jnp.int32

D_MODEL = 1024
HEAD_DIM = 64
N_HEADS = 8
WIDTH = N_HEADS * HEAD_DIM
KV_WIDTH = 2 * HEAD_DIM
IN_COLS = 3 * WIDTH + WIDTH + 2 * KV_WIDTH
ROPE_THETA = 500000.0
ROPE_DIMS_A = 16
AXIAL_THETA = 10000.0
GRID_W = 64
N_GROUPS = 4
EXPERTS_PER_GROUP = 8
N_EXPERTS = 32
D_EXPERT = 512
EPS = 1e-6
NEG = -1e30
DIL_W = 64
A_BATCH = 4

LANES = 128
VMEM_LIMIT = 56 * 1024 * 1024

TM_PROJ = 512
TILE_A = 1024
TQ_B = 128
COL_GROUP = 256
LOG2E = 1.4426950408889634
ROW_TILES = D_MODEL // LANES
BM_MOE = 256
TM_FIN = 512


def _cparams(sem):
    return pltpu.CompilerParams(dimension_semantics=sem, vmem_limit_bytes=VMEM_LIMIT)


def _mod_kernel(c_ref, w_ref, b_ref, o_ref):
    c = c_ref[...]
    a = c / (1.0 + jnp.exp(-c))
    o_ref[...] = jnp.dot(a, w_ref[...], preferred_element_type=F32,
                         precision=lax.Precision.HIGHEST) + b_ref[...]


def _modulation(c_pad, w, b):
    rows, d = c_pad.shape
    n = w.shape[1]
    tn = 1024
    return pl.pallas_call(
        _mod_kernel,
        grid=(n // tn,),
        in_specs=[pl.BlockSpec((rows, d), lambda j: (0, 0)),
                  pl.BlockSpec((d, tn), lambda j: (0, j)),
                  pl.BlockSpec((1, tn), lambda j: (0, j))],
        out_specs=pl.BlockSpec((rows, tn), lambda j: (0, j)),
        out_shape=jax.ShapeDtypeStruct((rows, n), F32),
        compiler_params=_cparams(("arbitrary",)),
        name="modulation",
    )(c_pad, w, b.reshape(1, n))


def _rope_tables(seq):
    t = jnp.arange(seq)
    tf = t.astype(F32)
    inv_a = ROPE_THETA ** (-jnp.arange(0, ROPE_DIMS_A, 2, dtype=F32) / ROPE_DIMS_A)
    ang = tf[:, None] * inv_a[None, :]
    cos, sin = jnp.cos(ang), jnp.sin(ang)
    rest = HEAD_DIM - ROPE_DIMS_A
    one = jnp.ones((seq, rest), F32)
    zero = jnp.zeros((seq, rest), F32)
    z8 = jnp.zeros_like(sin)
    cos64 = jnp.concatenate([cos, cos, one], 1)
    sm64 = jnp.concatenate([-sin, z8, zero], 1)
    sp64 = jnp.concatenate([z8, sin, zero], 1)
    ta = jnp.stack([jnp.tile(a, (1, 2)) for a in (cos64, sm64, sp64)])

    n_ax = HEAD_DIM // 2
    inv_b = AXIAL_THETA ** (-jnp.arange(0, n_ax, 2, dtype=F32) / n_ax)
    row_pos = (t // GRID_W).astype(F32)
    col_pos = (t % GRID_W).astype(F32)
    ar = row_pos[:, None] * inv_b[None, :]
    ac = col_pos[:, None] * inv_b[None, :]
    z16 = jnp.zeros_like(ar)
    cos64 = jnp.concatenate([jnp.cos(ar), jnp.cos(ar), jnp.cos(ac), jnp.cos(ac)], 1)
    sm64 = jnp.concatenate([-jnp.sin(ar), z16, -jnp.sin(ac), z16], 1)
    sp64 = jnp.concatenate([z16, jnp.sin(ar), z16, jnp.sin(ac)], 1)
    tb = jnp.stack([jnp.tile(a, (1, 2)) for a in (cos64, sm64, sp64)])
    return ta, tb


def _inproj_kernel(x_ref, mod_ref, g_ref, w_ref, ta_ref, tb_ref, qg_ref, kg_ref, bd_ref,
                   qa_ref, ka_ref, va_ref, qb_ref, kb_ref, vb_ref):
    x = x_ref[...]
    mod = mod_ref[0]
    inv = lax.rsqrt(jnp.mean(x * x, axis=-1, keepdims=True) + EPS)
    h = (x * inv) * (g_ref[...] * (1.0 + mod[1:2])) + mod[0:1]
    hb = h.astype(BF16)
    cos_a, sm_a, sp_a = ta_ref[0], ta_ref[1], ta_ref[2]
    cos_b, sm_b, sp_b = tb_ref[0], tb_ref[1], tb_ref[2]
    bd = bd_ref[...]
    lane = lax.broadcasted_iota(I32, (1, LANES), 1)
    low = lane < HEAD_DIM

    def proj(lo, n):
        return jnp.dot(hb, w_ref[:, lo:lo + n], preferred_element_type=F32)

    def rope(y, cos, sm, sp, sh):
        return y * cos + pltpu.roll(y, LANES - sh, 1) * sm + pltpu.roll(y, sh, 1) * sp

    def headnorm(y, g):
        t = y * y
        thi = t.astype(BF16)
        tlo = (t - thi.astype(F32)).astype(BF16)
        ss = jnp.dot(jnp.concatenate([thi, tlo], axis=1), bd, preferred_element_type=F32)
        return y * lax.rsqrt(ss * (1.0 / HEAD_DIM) + EPS) * g

    scale = HEAD_DIM ** -0.5
    qa = proj(0, WIDTH)
    ka = proj(WIDTH, WIDTH)
    for c in range(WIDTH // LANES):
        sl = slice(c * LANES, (c + 1) * LANES)
        qa_ref[:, sl] = rope(qa[:, sl], cos_a, sm_a, sp_a, 8) * scale
        ka_ref[:, sl] = rope(ka[:, sl], cos_a, sm_a, sp_a, 8)
    va_ref[...] = proj(2 * WIDTH, WIDTH)

    qb = proj(3 * WIDTH, WIDTH)
    qg = qg_ref[...]
    for c in range(WIDTH // LANES):
        y = rope(headnorm(qb[:, c * LANES:(c + 1) * LANES], qg), cos_b, sm_b, sp_b, 16) * (scale * LOG2E)
        ysw = pltpu.roll(y, HEAD_DIM, 1)
        grp = c // 2
        h0 = y if grp == 0 else ysw
        h1 = ysw if grp == 0 else y
        keep = low if grp == 0 else jnp.logical_not(low)
        qb_ref[2 * c] = jnp.where(keep, h0, 0.0).T.astype(BF16)
        qb_ref[2 * c + 1] = jnp.where(keep, h1, 0.0).T.astype(BF16)
    kb = proj(4 * WIDTH, KV_WIDTH)
    kb_ref[...] = rope(headnorm(kb, kg_ref[...]), cos_b, sm_b, sp_b, 16).astype(BF16)
    vb_ref[0] = proj(4 * WIDTH + KV_WIDTH, KV_WIDTH).T.astype(BF16)


def _inproj(x2d, mod, boff, seq, ln1_g, w_in_bf, ta, tb, qg, kg, bd):
    t_tok = x2d.shape[0]
    tm = TM_PROJ
    per_seq = seq // tm
    tok = lambda i: (i, 0)
    const = lambda i: (0, 0)
    return pl.pallas_call(
        _inproj_kernel,
        grid=(t_tok // tm,),
        in_specs=[
            pl.BlockSpec((tm, D_MODEL), tok),
            pl.BlockSpec((1, 6, D_MODEL), lambda i: (boff + i // per_seq, 0, 0)),
            pl.BlockSpec((1, D_MODEL), const),
            pl.BlockSpec((D_MODEL, IN_COLS), const),
            pl.BlockSpec((3, tm, LANES), lambda i: (0, i % per_seq, 0)),
            pl.BlockSpec((3, tm, LANES), lambda i: (0, i % per_seq, 0)),
            pl.BlockSpec((1, LANES), const),
            pl.BlockSpec((1, LANES), const),
            pl.BlockSpec((2 * LANES, LANES), const),
        ],
        out_specs=[
            pl.BlockSpec((tm, WIDTH), tok),
            pl.BlockSpec((tm, WIDTH), tok),
            pl.BlockSpec((tm, WIDTH), tok),
            pl.BlockSpec((N_HEADS, LANES, tm), lambda i: (0, 0, i)),
            pl.BlockSpec((tm, KV_WIDTH), tok),
            pl.BlockSpec((1, KV_WIDTH, tm), lambda i: (i, 0, 0)),
        ],
        out_shape=[
            jax.ShapeDtypeStruct((t_tok, WIDTH), F32),
            jax.ShapeDtypeStruct((t_tok, WIDTH), F32),
            jax.ShapeDtypeStruct((t_tok, WIDTH), F32),
            jax.ShapeDtypeStruct((N_HEADS, LANES, t_tok), BF16),
            jax.ShapeDtypeStruct((t_tok, KV_WIDTH), BF16),
            jax.ShapeDtypeStruct((t_tok // tm, KV_WIDTH, tm), BF16),
        ],
        compiler_params=_cparams(("parallel",)),
        name="inproj",
    )(x2d, mod, ln1_g.reshape(1, D_MODEL), w_in_bf, ta, tb, qg, kg, bd)


def _attn_a_kernel(q_ref, kp_ref, kc_ref, kn_ref, vp_ref, vc_ref, vn_ref, o_ref,
                   m1, l1, a1, m2, l2, a2, m3, l3, a3, *, seq):
    n = pl.program_id(1)
    t = TILE_A
    w = DIL_W

    lane = lax.broadcasted_iota(I32, (1, LANES), 1)
    low = lane < HEAD_DIM
    row_i = lax.broadcasted_iota(I32, (2 * w, 1), 0) & (w - 1)
    col_m = lax.broadcasted_iota(I32, (1, 3 * w), 1)
    band = jnp.abs(col_m - w - row_i) <= w
    base = n * t - t

    def window_rows(refs, start, count, d):
        parts = []
        while count > 0:
            ref = refs[start // t]
            local = start % t
            take = min(count, -(-(t - local) // d))
            parts.append(ref[pl.ds(local, take, stride=d), :] if d > 1 else ref[pl.ds(local, take), :])
            start += take * d
            count -= take
        return parts[0] if len(parts) == 1 else jnp.concatenate(parts, axis=0)

    units = []
    for d, m_s, l_s, a_s in ((1, m1, l1, a1), (4, m2, l2, a2), (16, m3, l3, a3)):
        for j in range(t // w):
            qs = (j // d) * (w * d) + (j % d)
            units.append((d, qs, m_s, l_s, a_s))

    def rows(qs, d):
        return pl.ds(qs, w, stride=d) if d > 1 else pl.ds(qs, w)

    def score(unit):
        d, qs = unit[0], unit[1]
        ks = qs + t - w * d
        q = q_ref[rows(qs, d), :]
        k = window_rows((kp_ref, kc_ref, kn_ref), ks, 3 * w, d).astype(BF16)
        q2 = jnp.concatenate([jnp.where(low, q, 0.0), jnp.where(low, 0.0, q)], axis=0).astype(BF16)
        s = lax.dot_general(q2, k, (((1,), (1,)), ((), ())), preferred_element_type=F32)
        kpos = base + ks + d * col_m
        valid = band & (kpos >= 0) & (kpos < seq)
        return jnp.where(valid, s, NEG)

    def finish(unit, s):
        d, qs, m_s, l_s, a_s = unit
        ks = qs + t - w * d
        v = window_rows((vp_ref, vc_ref, vn_ref), ks, 3 * w, d).astype(BF16)
        mx = jnp.max(s, axis=-1, keepdims=True)
        p = jnp.exp(s - mx)
        den = jnp.sum(p, axis=-1, keepdims=True)
        o = jnp.dot(p.astype(BF16), v, preferred_element_type=F32)
        m_s[rows(qs, d), :] = jnp.where(low, mx[0:w], mx[w:2 * w])
        l_s[rows(qs, d), :] = jnp.where(low, den[0:w], den[w:2 * w])
        a_s[rows(qs, d), :] = jnp.where(low, o[0:w], o[w:2 * w])

    nb = A_BATCH
    pending = [score(u) for u in units[:nb]]
    for b0 in range(0, len(units), nb):
        ahead = [score(u) for u in units[b0 + nb:b0 + 2 * nb]]
        for u, s in zip(units[b0:b0 + nb], pending):
            finish(u, s)
        pending = ahead

    mm = jnp.maximum(jnp.maximum(m1[...], m2[...]), m3[...])
    w1 = jnp.exp(m1[...] - mm)
    w2 = jnp.exp(m2[...] - mm)
    w3 = jnp.exp(m3[...] - mm)
    num = w1 * a1[...] + w2 * a2[...] + w3 * a3[...]
    den = w1 * l1[...] + w2 * l2[...] + w3 * l3[...]
    o_ref[...] = (num / den).astype(o_ref.dtype)


def _attn_a(qa, ka, va, batch, seq):
    t = TILE_A
    nt = seq // t
    q3 = qa.reshape(batch, seq, WIDTH)
    k3 = ka.reshape(batch, seq, WIDTH)
    v3 = va.reshape(batch, seq, WIDTH)
    cur = lambda b, n, p: (b, n, p)
    prev = lambda b, n, p: (b, jnp.maximum(n - 1, 0), p)
    nxt = lambda b, n, p: (b, jnp.minimum(n + 1, nt - 1), p)
    blk = (None, t, LANES)
    scr = pltpu.VMEM((t, LANES), F32)
    out = pl.pallas_call(
        functools.partial(_attn_a_kernel, seq=seq),
        grid=(batch, nt, WIDTH // LANES),
        in_specs=[pl.BlockSpec(blk, cur),
                  pl.BlockSpec(blk, prev), pl.BlockSpec(blk, cur), pl.BlockSpec(blk, nxt),
                  pl.BlockSpec(blk, prev), pl.BlockSpec(blk, cur), pl.BlockSpec(blk, nxt)],
        out_specs=pl.BlockSpec(blk, cur),
        out_shape=jax.ShapeDtypeStruct((batch, seq, WIDTH), BF16),
        scratch_shapes=[scr] * 9,
        compiler_params=_cparams(("parallel", "parallel", "parallel")),
        name="attn_dilated",
    )(q3, k3, k3, k3, v3, v3, v3)
    return out.reshape(batch * seq, WIDTH)


def _attn_b_kernel(qt_ref, k_ref, vt_ref, o_ref, m_s, l_s, acc_s, s_a, s_b, *, tq, nkv):
    m_s[...] = jnp.full(m_s.shape, NEG, F32)
    l_s[...] = jnp.zeros(l_s.shape, F32)
    acc_s[...] = jnp.zeros(acc_s.shape, F32)
    per = COL_GROUP // tq
    ngrp = N_HEADS // per
    qg = [jnp.concatenate([qt_ref[per * g + u] for u in range(per)], axis=1) for g in range(ngrp)]
    tk = k_ref.shape[1]
    fold = 8

    def scores(j, dst):
        k = k_ref[j]
        for g in range(ngrp):
            dst[g] = jnp.dot(k, qg[g], preferred_element_type=F32)

    def update(j, src):
        vt = vt_ref[j]
        for g in range(ngrp):
            cs = slice(COL_GROUP * g, COL_GROUP * (g + 1))
            s = src[g]
            part = jnp.max(s.reshape(fold, tk // fold, COL_GROUP), axis=0)
            m_old = m_s[:, cs]
            m_new = jnp.maximum(m_old, jnp.max(part, axis=0, keepdims=True))
            alpha = jnp.exp2(m_old - m_new)
            p = jnp.exp2(s - m_new)
            psum = jnp.sum(jnp.sum(p.reshape(fold, tk // fold, COL_GROUP), axis=0), axis=0, keepdims=True)
            l_s[:, cs] = alpha * l_s[:, cs] + psum
            acc_s[:, cs] = alpha * acc_s[:, cs] + jnp.dot(vt, p.astype(BF16), preferred_element_type=F32)
            m_s[:, cs] = m_new

    scores(0, s_a)

    def body(pair, carry):
        j = 2 * pair
        scores(j + 1, s_b)
        update(j, s_a)
        scores(j + 2, s_a)
        update(j + 1, s_b)
        return carry

    lax.fori_loop(0, nkv // 2 - 1, body, 0)
    scores(nkv - 1, s_b)
    update(nkv - 2, s_a)
    update(nkv - 1, s_b)

    o = acc_s[...] / l_s[...]
    for c in range(N_HEADS // 2):
        r0 = HEAD_DIM * (c // 2)
        pair = jnp.concatenate([o[r0:r0 + HEAD_DIM, (2 * c) * tq:(2 * c + 1) * tq],
                                o[r0:r0 + HEAD_DIM, (2 * c + 1) * tq:(2 * c + 2) * tq]], axis=0)
        o_ref[:, c * LANES:(c + 1) * LANES] = pair.T.astype(o_ref.dtype)


def _attn_b(qbt, kb, vbt, batch, seq):
    tq, tk = TQ_B, TM_PROJ
    nq = seq // tq
    nkv = seq // tk
    assert nkv % 2 == 0 and nkv >= 2
    ngrp = N_HEADS * tq // COL_GROUP
    sbuf = pltpu.VMEM((ngrp, tk, COL_GROUP), F32)
    k3 = kb.reshape(batch * nkv, tk, KV_WIDTH)
    return pl.pallas_call(
        functools.partial(_attn_b_kernel, tq=tq, nkv=nkv),
        grid=(batch, nq),
        in_specs=[pl.BlockSpec((N_HEADS, LANES, tq), lambda b, i: (0, 0, b * nq + i)),
                  pl.BlockSpec((nkv, tk, KV_WIDTH), lambda b, i: (b, 0, 0)),
                  pl.BlockSpec((nkv, KV_WIDTH, tk), lambda b, i: (b, 0, 0))],
        out_specs=pl.BlockSpec((tq, WIDTH), lambda b, i: (b * nq + i, 0)),
        out_shape=jax.ShapeDtypeStruct((batch * seq, WIDTH), BF16),
        scratch_shapes=[pltpu.VMEM((1, N_HEADS * tq), F32), pltpu.VMEM((1, N_HEADS * tq), F32),
                        pltpu.VMEM((LANES, N_HEADS * tq), F32), sbuf, sbuf],
        compiler_params=_cparams(("parallel", "parallel")),
        name="attn_dense",
    )(qbt, k3, vbt)


def _outproj_kernel(x_ref, oa_ref, ob_ref, mod_ref, ga_ref, gb_ref, wo_ref, ln2_ref,
                    wrh_ref, wrl_ref, br_ref, x1_ref, h2_ref, rt_ref):
    def rms(o, g):
        return o * lax.rsqrt(jnp.mean(o * o, axis=-1, keepdims=True) + EPS) * g

    oa = rms(oa_ref[...].astype(F32), ga_ref[...])
    ob = rms(ob_ref[...].astype(F32), gb_ref[...])
    o = jnp.concatenate([oa, ob], axis=1).astype(BF16)
    mod = mod_ref[0]
    x1 = x_ref[...] + mod[2:3] * jnp.dot(o, wo_ref[...], preferred_element_type=F32)
    x1_ref[...] = x1
    inv = lax.rsqrt(jnp.mean(x1 * x1, axis=-1, keepdims=True) + EPS)
    h2 = (x1 * inv) * (ln2_ref[...] * (1.0 + mod[4:5])) + mod[3:4]
    tm = h2.shape[0]
    for c in range(ROW_TILES):
        h2_ref[pl.ds(c, tm, stride=ROW_TILES), :] = h2[:, c * LANES:(c + 1) * LANES]

    hh = h2.astype(BF16)
    hl = (h2 - hh.astype(F32)).astype(BF16)
    wrh = wrh_ref[...]
    logits = (jnp.dot(hh, wrh, preferred_element_type=F32)
              + jnp.dot(hl, wrh, preferred_element_type=F32)
              + jnp.dot(hh, wrl_ref[...], preferred_element_type=F32)) + br_ref[...]

    lane = lax.broadcasted_iota(I32, logits.shape, 1).astype(F32)
    big = jnp.float32(LANES)
    is_g = lane < N_GROUPS
    gl = jnp.where(is_g, logits, -jnp.inf)
    gmax = jnp.max(gl, axis=-1, keepdims=True)
    g_sel = jnp.min(jnp.where(gl == gmax, lane, big), axis=-1, keepdims=True)
    g_den = jnp.sum(jnp.exp(gl - gmax), axis=-1, keepdims=True)
    g_w = 1.0 / g_den
    e_lo = N_GROUPS + EXPERTS_PER_GROUP * g_sel
    is_e = (lane >= e_lo) & (lane < e_lo + EXPERTS_PER_GROUP)
    el = jnp.where(is_e, logits, -jnp.inf)
    emax = jnp.max(el, axis=-1, keepdims=True)
    p = jnp.where(is_e, jnp.exp(el - emax), -1.0)
    p1 = jnp.max(p, axis=-1, keepdims=True)
    i1 = jnp.min(jnp.where(p == p1, lane, big), axis=-1, keepdims=True)
    pm = jnp.where(lane == i1, -1.0, p)
    p2 = jnp.max(pm, axis=-1, keepdims=True)
    i2 = jnp.min(jnp.where(pm == p2, lane, big), axis=-1, keepdims=True)
    tot = p1 + p2
    gate1 = g_w * p1 / tot
    gate2 = g_w * p2 / tot
    e1 = i1 - N_GROUPS
    e2 = i2 - N_GROUPS
    rt_ref[...] = jnp.where(lane == 0, gate1, jnp.where(lane == 1, gate2,
                            jnp.where(lane == 2, e1, jnp.where(lane == 3, e2, 0.0))))


def _outproj(x2d, oa, ob, mod, boff, seq, ga, gb, wo_bf, ln2_g, wrh, wrl, br):
    t_tok = x2d.shape[0]
    tm = TM_PROJ
    per_seq = seq // tm
    tok = lambda i: (i, 0)
    const = lambda i: (0, 0)
    return pl.pallas_call(
        _outproj_kernel,
        grid=(t_tok // tm,),
        in_specs=[
            pl.BlockSpec((tm, D_MODEL), tok),
            pl.BlockSpec((tm, WIDTH), tok),
            pl.BlockSpec((tm, WIDTH), tok),
            pl.BlockSpec((1, 6, D_MODEL), lambda i: (boff + i // per_seq, 0, 0)),
            pl.BlockSpec((1, WIDTH), const),
            pl.BlockSpec((1, WIDTH), const),
            pl.BlockSpec((D_MODEL, D_MODEL), const),
            pl.BlockSpec((1, D_MODEL), const),
            pl.BlockSpec((D_MODEL, LANES), const),
            pl.BlockSpec((D_MODEL, LANES), const),
            pl.BlockSpec((1, LANES), const),
        ],
        out_specs=[pl.BlockSpec((tm, D_MODEL), tok),
                   pl.BlockSpec((tm * ROW_TILES, LANES), tok),
                   pl.BlockSpec((tm, LANES), tok)],
        out_shape=[jax.ShapeDtypeStruct((t_tok, D_MODEL), F32),
                   jax.ShapeDtypeStruct((t_tok * ROW_TILES, LANES), F32),
                   jax.ShapeDtypeStruct((t_tok, LANES), F32)],
        compiler_params=_cparams(("parallel",)),
        name="outproj_router",
    )(x2d, oa, ob, mod, ga.reshape(1, WIDTH), gb.reshape(1, WIDTH), wo_bf,
      ln2_g.reshape(1, D_MODEL), wrh, wrl, br)


def _moe_kernel(be_ref, tokc_ref, tokn_ref, row_ref, h2_hbm, w1_ref, w3_ref, w2_ref, y_hbm,
                xbuf, ybuf, gsem, ssem):
    del be_ref
    i = pl.program_id(0)
    n = pl.num_programs(0)
    bm = BM_MOE
    slot = i % 2

    rt = ROW_TILES

    def start_gather(tok_ref, dst_slot):
        def body(r, carry):
            src = pl.multiple_of(tok_ref[0, 0, r] * rt, rt)
            pltpu.make_async_copy(h2_hbm.at[pl.ds(src, rt)],
                                  xbuf.at[dst_slot, pl.ds(pl.multiple_of(r * rt, rt), rt)],
                                  gsem.at[dst_slot]).start()
            return carry
        lax.fori_loop(0, bm, body, 0, unroll=8)

    def wait_gather(s):
        pltpu.make_async_copy(h2_hbm.at[pl.ds(0, bm * rt)], xbuf.at[s], gsem.at[s]).wait()

    def wait_scatter(s):
        pltpu.make_async_copy(ybuf.at[s], y_hbm.at[pl.ds(0, bm * rt)], ssem.at[s]).wait()

    @pl.when(i == 0)
    def _():
        start_gather(tokc_ref, 0)

    @pl.when(i + 1 < n)
    def _():
        start_gather(tokn_ref, 1 - slot)

    wait_gather(slot)
    x = jnp.concatenate([xbuf[slot, pl.ds(c, bm, stride=rt), :] for c in range(rt)], axis=1).astype(BF16)
    a = jnp.dot(x, w1_ref[...], preferred_element_type=F32)
    b = jnp.dot(x, w3_ref[...], preferred_element_type=F32)
    hmid = ((a / (1.0 + jnp.exp(-a))) * b).astype(BF16)
    y = jnp.dot(hmid, w2_ref[...], preferred_element_type=F32)

    @pl.when(i >= 2)
    def _():
        wait_scatter(slot)

    for c in range(rt):
        ybuf[slot, pl.ds(c, bm, stride=rt), :] = y[:, c * LANES:(c + 1) * LANES]

    def scatter(r, carry):
        dst = pl.multiple_of(row_ref[0, 0, r] * rt, rt)
        pltpu.make_async_copy(ybuf.at[slot, pl.ds(pl.multiple_of(r * rt, rt), rt)],
                              y_hbm.at[pl.ds(dst, rt)], ssem.at[slot]).start()
        return carry
    lax.fori_loop(0, bm, scatter, 0, unroll=8)

    @pl.when(i == n - 1)
    def _():
        wait_scatter(slot)
        wait_scatter(1 - slot)


def _moe(h2, block_e, slot_tok, out_row, w1_bf, w3_bf, w2_bf):
    bm = BM_MOE
    n_blocks = block_e.shape[0]
    p_rows = n_blocks * bm
    tok3 = slot_tok.reshape(n_blocks, 1, bm)
    row3 = out_row.reshape(n_blocks, 1, bm)
    smem_blk = lambda fn: pl.BlockSpec((1, 1, bm), fn, memory_space=pltpu.SMEM)
    grid_spec = pltpu.PrefetchScalarGridSpec(
        num_scalar_prefetch=1,
        grid=(n_blocks,),
        in_specs=[
            smem_blk(lambda i, be: (i, 0, 0)),
            smem_blk(lambda i, be: (jnp.minimum(i + 1, n_blocks - 1), 0, 0)),
            smem_blk(lambda i, be: (i, 0, 0)),
            pl.BlockSpec(memory_space=pl.ANY),
            pl.BlockSpec((None, D_MODEL, D_EXPERT), lambda i, be: (be[i], 0, 0)),
            pl.BlockSpec((None, D_MODEL, D_EXPERT), lambda i, be: (be[i], 0, 0)),
            pl.BlockSpec((None, D_EXPERT, D_MODEL), lambda i, be: (be[i], 0, 0)),
        ],
        out_specs=pl.BlockSpec(memory_space=pl.ANY),
        scratch_shapes=[pltpu.VMEM((2, bm * ROW_TILES, LANES), F32), pltpu.VMEM((2, bm * ROW_TILES, LANES), F32),
                        pltpu.SemaphoreType.DMA((2,)), pltpu.SemaphoreType.DMA((2,))],
    )
    return pl.pallas_call(
        _moe_kernel,
        grid_spec=grid_spec,
        out_shape=jax.ShapeDtypeStruct((p_rows * ROW_TILES, LANES), F32),
        compiler_params=_cparams(("arbitrary",)),
        name="moe_experts",
    )(block_e, tok3, tok3, row3, h2, w1_bf, w3_bf, w2_bf)


def _moe_plan(expert, bm):
    t_tok = expert.shape[0]
    a_tot = 2 * t_tok
    flat_e = expert.reshape(-1)
    onehot = (flat_e[:, None] == jnp.arange(N_EXPERTS, dtype=I32)[None, :]).astype(I32)
    csum = jnp.cumsum(onehot, axis=0)
    rank = jnp.take_along_axis(csum, flat_e[:, None], axis=1)[:, 0] - 1
    counts = csum[-1]
    pcounts = (counts + bm - 1) // bm * bm
    pends = jnp.cumsum(pcounts)
    pstarts = pends - pcounts
    dest = pstarts[flat_e] + rank
    n_blocks = a_tot // bm + N_EXPERTS
    p_rows = n_blocks * bm
    slot_assign = jnp.full((p_rows,), -1, I32).at[dest].set(jnp.arange(a_tot, dtype=I32))
    is_pad = slot_assign < 0
    pad_rank = jnp.cumsum(is_pad.astype(I32)) - 1
    out_row = jnp.where(is_pad, a_tot + pad_rank, slot_assign).astype(I32)
    slot_tok = jnp.where(is_pad, 0, slot_assign // 2).astype(I32)
    block_e = jnp.minimum(
        jnp.searchsorted(pends, jnp.arange(n_blocks, dtype=I32) * bm, side="right"), N_EXPERTS - 1).astype(I32)
    return block_e, slot_tok, out_row


def _final_kernel(x1_ref, y_ref, rt_ref, mod_ref, modf_ref, g_ref, o_ref):
    rt = rt_ref[...]
    tm = rt.shape[0]
    g1, g2 = rt[:, 0:1], rt[:, 1:2]
    moe = jnp.concatenate(
        [g1 * y_ref[pl.ds(c, tm, stride=2 * ROW_TILES), :]
         + g2 * y_ref[pl.ds(ROW_TILES + c, tm, stride=2 * ROW_TILES), :] for c in range(ROW_TILES)], axis=1)
    x2 = x1_ref[...] + mod_ref[0][5:6] * moe
    modf = modf_ref[0]
    inv = lax.rsqrt(jnp.mean(x2 * x2, axis=-1, keepdims=True) + EPS)
    o_ref[...] = (x2 * inv) * (g_ref[...] * (1.0 + modf[1:2])) + modf[0:1]


def _final(x1, y2, rt, mod, modf, boff, seq, lnf_g):
    t_tok = x1.shape[0]
    tm = TM_FIN
    per_seq = seq // tm
    tok = lambda i: (i, 0)
    return pl.pallas_call(
        _final_kernel,
        grid=(t_tok // tm,),
        in_specs=[pl.BlockSpec((tm, D_MODEL), tok),
                  pl.BlockSpec((tm * 2 * ROW_TILES, LANES), tok),
                  pl.BlockSpec((tm, LANES), tok),
                  pl.BlockSpec((1, 6, D_MODEL), lambda i: (boff + i // per_seq, 0, 0)),
                  pl.BlockSpec((1, 2, D_MODEL), lambda i: (boff + i // per_seq, 0, 0)),
                  pl.BlockSpec((1, D_MODEL), lambda i: (0, 0))],
        out_specs=pl.BlockSpec((tm, D_MODEL), tok),
        out_shape=jax.ShapeDtypeStruct((t_tok, D_MODEL), F32),
        compiler_params=_cparams(("parallel",)),
        name="combine_final",
    )(x1, y2, rt, mod, modf, lnf_g.reshape(1, D_MODEL))


def _trunk(x, boff, mod, modf, prm):
    batch, seq, _ = x.shape
    x2d = x.reshape(batch * seq, D_MODEL)
    ta, tb = _rope_tables(seq)
    qa, ka, va, qb8, kb, vb = _inproj(x2d, mod, boff, seq, prm["ln1_g"], prm["w_in"], ta, tb,
                                      prm["qg"], prm["kg"], prm["bd"])
    oa = _attn_a(qa, ka, va, batch, seq)
    ob = _attn_b(qb8, kb, vb, batch, seq)
    x1, h2, rt = _outproj(x2d, oa, ob, mod, boff, seq, prm["on_a_g"], prm["on_b_g"], prm["w_out"],
                          prm["ln2_g"], prm["wrh"], prm["wrl"], prm["br"])
    expert = rt[:, 2:4].astype(I32)
    block_e, slot_tok, out_row = _moe_plan(expert, BM_MOE)
    y2 = _moe(h2, block_e, slot_tok, out_row, prm["w1"], prm["w3"], prm["w2"])
    y = _final(x1, y2, rt, mod, modf, boff, seq, prm["lnf_g"])
    return y.reshape(batch, seq, D_MODEL)


def kernel(x_prompt, x_sample, c_prompt, c_sample, ln1_g, ln2_g, w_ada, b_ada, w_in, w_out, qn_g, kn_g,
           on_a_g, on_b_g, w_rg, b_rg, w_re, b_re, w1, w3, w2, lnf_g, w_adaf, b_adaf):
    assert ln1_g.shape[0] == 1, "single-layer trunk"
    nb_p, nb_s = c_prompt.shape[0], c_sample.shape[0]
    rows = -(-(nb_p + nb_s) // 8) * 8
    c_all = jnp.concatenate([c_prompt, c_sample, jnp.zeros((rows - nb_p - nb_s, D_MODEL), F32)], axis=0)
    mod = _modulation(c_all, w_ada[0], b_ada[0]).reshape(rows, 6, D_MODEL)
    modf = _modulation(c_all, w_adaf, b_adaf).reshape(rows, 2, D_MODEL)

    wr = jnp.concatenate([w_rg[0], w_re[0], jnp.zeros((D_MODEL, LANES - N_GROUPS - N_EXPERTS), F32)], axis=1)
    wrh = wr.astype(BF16)
    wrl = (wr - wrh.astype(F32)).astype(BF16)
    br = jnp.concatenate([b_rg[0], b_re[0], jnp.zeros((LANES - N_GROUPS - N_EXPERTS,), F32)]).reshape(1, LANES)
    seg = lax.broadcasted_iota(I32, (2 * LANES, LANES), 0) % LANES // HEAD_DIM
    bd = (seg == lax.broadcasted_iota(I32, (2 * LANES, LANES), 1) // HEAD_DIM).astype(BF16)
    prm = dict(
        ln1_g=ln1_g[0], ln2_g=ln2_g[0], w_in=w_in[0].astype(BF16), w_out=w_out[0].astype(BF16),
        qg=jnp.tile(qn_g[0], 2).reshape(1, LANES), kg=jnp.tile(kn_g[0], 2).reshape(1, LANES), bd=bd,
        on_a_g=on_a_g[0], on_b_g=on_b_g[0], wrh=wrh, wrl=wrl, br=br,
        w1=w1[0].astype(BF16), w3=w3[0].astype(BF16), w2=w2[0].astype(BF16), lnf_g=lnf_g,
    )
    y_prompt = _trunk(x_prompt, 0, mod, modf, prm)
    y_sample = _trunk(x_sample, nb_p, mod, modf, prm)
    return (y_prompt, y_sample)
```

```python
import functools

import jax
import jax.numpy as jnp
from jax import lax
from jax.experimental import pallas as pl
from jax.experimental.pallas import tpu as pltpu

F32 = jnp.float32
BF16 = jnp.bfloat16
I32 = jnp.int32

D_MODEL = 1024
HEAD_DIM = 64
N_HEADS = 8
WIDTH = N_HEADS * HEAD_DIM
KV_WIDTH = 2 * HEAD_DIM
IN_COLS = 3 * WIDTH + WIDTH + 2 * KV_WIDTH
ROPE_THETA = 500000.0
ROPE_DIMS_A = 16
AXIAL_THETA = 10000.0
GRID_W = 64
N_GROUPS = 4
EXPERTS_PER_GROUP = 8
N_EXPERTS = 32
D_EXPERT = 512
EPS = 1e-6
NEG = -1e30
DIL_W = 64
A_BATCH = 4

LANES = 128
VMEM_LIMIT = 56 * 1024 * 1024

TM_PROJ = 512
TILE_A = 1024
TQ_B = 256
COL_GROUP = 256
BF16_ROWS = 16
LOG2E = 1.4426950408889634
ROW_TILES = D_MODEL // LANES
BM_MOE = 256
PERM_CHUNK = 1024
TM_FIN = 512


def _cparams(sem):
    return pltpu.CompilerParams(dimension_semantics=sem, vmem_limit_bytes=VMEM_LIMIT)


def _mod_kernel(c_ref, w_ref, b_ref, o_ref):
    c = c_ref[...]
    a = c / (1.0 + jnp.exp(-c))
    o_ref[...] = jnp.dot(a, w_ref[...], preferred_element_type=F32,
                         precision=lax.Precision.HIGHEST) + b_ref[...]


def _modulation(c_pad, w, b):
    rows, d = c_pad.shape
    n = w.shape[1]
    tn = 1024
    return pl.pallas_call(
        _mod_kernel,
        grid=(n // tn,),
        in_specs=[pl.BlockSpec((rows, d), lambda j: (0, 0)),
                  pl.BlockSpec((d, tn), lambda j: (0, j)),
                  pl.BlockSpec((1, tn), lambda j: (0, j))],
        out_specs=pl.BlockSpec((rows, tn), lambda j: (0, j)),
        out_shape=jax.ShapeDtypeStruct((rows, n), F32),
        compiler_params=_cparams(("arbitrary",)),
        name="modulation",
    )(c_pad, w, b.reshape(1, n))


def _rope_tables(seq):
    t = jnp.arange(seq)
    tf = t.astype(F32)
    inv_a = ROPE_THETA ** (-jnp.arange(0, ROPE_DIMS_A, 2, dtype=F32) / ROPE_DIMS_A)
    ang = tf[:, None] * inv_a[None, :]
    cos, sin = jnp.cos(ang), jnp.sin(ang)
    rest = HEAD_DIM - ROPE_DIMS_A
    one = jnp.ones((seq, rest), F32)
    zero = jnp.zeros((seq, rest), F32)
    z8 = jnp.zeros_like(sin)
    cos64 = jnp.concatenate([cos, cos, one], 1)
    sm64 = jnp.concatenate([-sin, z8, zero], 1)
    sp64 = jnp.concatenate([z8, sin, zero], 1)
    ta = jnp.stack([jnp.tile(a, (1, 2)) for a in (cos64, sm64, sp64)])

    n_ax = HEAD_DIM // 2
    inv_b = AXIAL_THETA ** (-jnp.arange(0, n_ax, 2, dtype=F32) / n_ax)
    row_pos = (t // GRID_W).astype(F32)
    col_pos = (t % GRID_W).astype(F32)
    ar = row_pos[:, None] * inv_b[None, :]
    ac = col_pos[:, None] * inv_b[None, :]
    z16 = jnp.zeros_like(ar)
    cos64 = jnp.concatenate([jnp.cos(ar), jnp.cos(ar), jnp.cos(ac), jnp.cos(ac)], 1)
    sm64 = jnp.concatenate([-jnp.sin(ar), z16, -jnp.sin(ac), z16], 1)
    sp64 = jnp.concatenate([z16, jnp.sin(ar), z16, jnp.sin(ac)], 1)
    tb = jnp.stack([jnp.tile(a, (1, 2)) for a in (cos64, sm64, sp64)])
    return ta, tb


def _inproj_kernel(x_ref, mod_ref, g_ref, w_ref, ta_ref, tb_ref, qg_ref, kg_ref, bd_ref,
                   qa_ref, ka_ref, va_ref, qb_ref, kb_ref, vb_ref):
    x = x_ref[...]
    mod = mod_ref[0]
    inv = lax.rsqrt(jnp.mean(x * x, axis=-1, keepdims=True) + EPS)
    h = (x * inv) * (g_ref[...] * (1.0 + mod[1:2])) + mod[0:1]
    hb = h.astype(BF16)
    cos_a, sm_a, sp_a = ta_ref[0], ta_ref[1], ta_ref[2]
    cos_b, sm_b, sp_b = tb_ref[0], tb_ref[1], tb_ref[2]
    bd = bd_ref[...]
    lane = lax.broadcasted_iota(I32, (1, LANES), 1)
    low = lane < HEAD_DIM

    def proj(lo, n):
        return jnp.dot(hb, w_ref[:, lo:lo + n], preferred_element_type=F32)

    def rope(y, cos, sm, sp, sh):
        return y * cos + pltpu.roll(y, LANES - sh, 1) * sm + pltpu.roll(y, sh, 1) * sp

    def headnorm(y, g):
        t = y * y
        thi = t.astype(BF16)
        tlo = (t - thi.astype(F32)).astype(BF16)
        ss = jnp.dot(jnp.concatenate([thi, tlo], axis=1), bd, preferred_element_type=F32)
        return y * lax.rsqrt(ss * (1.0 / HEAD_DIM) + EPS) * g

    scale = HEAD_DIM ** -0.5
    qa = proj(0, WIDTH)
    ka = proj(WIDTH, WIDTH)
    for c in range(WIDTH // LANES):
        sl = slice(c * LANES, (c + 1) * LANES)
        qa_ref[:, sl] = rope(qa[:, sl], cos_a, sm_a, sp_a, 8) * scale
        ka_ref[:, sl] = rope(ka[:, sl], cos_a, sm_a, sp_a, 8)
    va_ref[...] = proj(2 * WIDTH, WIDTH)

    qb = proj(3 * WIDTH, WIDTH)
    qg = qg_ref[...]
    for c in range(WIDTH // LANES):
        y = rope(headnorm(qb[:, c * LANES:(c + 1) * LANES], qg), cos_b, sm_b, sp_b, 16) * (scale * LOG2E)
        ysw = pltpu.roll(y, HEAD_DIM, 1)
        grp = c // 2
        h0 = y if grp == 0 else ysw
        h1 = ysw if grp == 0 else y
        keep = low if grp == 0 else jnp.logical_not(low)
        qb_ref[2 * c] = jnp.where(keep, h0, 0.0).T.astype(BF16)
        qb_ref[2 * c + 1] = jnp.where(keep, h1, 0.0).T.astype(BF16)
    kb = proj(4 * WIDTH, KV_WIDTH)
    kb_ref[...] = rope(headnorm(kb, kg_ref[...]), cos_b, sm_b, sp_b, 16).astype(BF16)
    vb_ref[0] = proj(4 * WIDTH + KV_WIDTH, KV_WIDTH).T.astype(BF16)


def _inproj(x2d, mod, boff, seq, ln1_g, w_in_bf, ta, tb, qg, kg, bd):
    t_tok = x2d.shape[0]
    tm = TM_PROJ
    per_seq = seq // tm
    tok = lambda i: (i, 0)
    const = lambda i: (0, 0)
    return pl.pallas_call(
        _inproj_kernel,
        grid=(t_tok // tm,),
        in_specs=[
            pl.BlockSpec((tm, D_MODEL), tok),
            pl.BlockSpec((1, 6, D_MODEL), lambda i: (boff + i // per_seq, 0, 0)),
            pl.BlockSpec((1, D_MODEL), const),
            pl.BlockSpec((D_MODEL, IN_COLS), const),
            pl.BlockSpec((3, tm, LANES), lambda i: (0, i % per_seq, 0)),
            pl.BlockSpec((3, tm, LANES), lambda i: (0, i % per_seq, 0)),
            pl.BlockSpec((1, LANES), const),
            pl.BlockSpec((1, LANES), const),
            pl.BlockSpec((2 * LANES, LANES), const),
        ],
        out_specs=[
            pl.BlockSpec((tm, WIDTH), tok),
            pl.BlockSpec((tm, WIDTH), tok),
            pl.BlockSpec((tm, WIDTH), tok),
            pl.BlockSpec((N_HEADS, LANES, tm), lambda i: (0, 0, i)),
            pl.BlockSpec((tm, KV_WIDTH), tok),
            pl.BlockSpec((1, KV_WIDTH, tm), lambda i: (i, 0, 0)),
        ],
        out_shape=[
            jax.ShapeDtypeStruct((t_tok, WIDTH), F32),
            jax.ShapeDtypeStruct((t_tok, WIDTH), F32),
            jax.ShapeDtypeStruct((t_tok, WIDTH), F32),
            jax.ShapeDtypeStruct((N_HEADS, LANES, t_tok), BF16),
            jax.ShapeDtypeStruct((t_tok, KV_WIDTH), BF16),
            jax.ShapeDtypeStruct((t_tok // tm, KV_WIDTH, tm), BF16),
        ],
        compiler_params=_cparams(("parallel",)),
        name="inproj",
    )(x2d, mod, ln1_g.reshape(1, D_MODEL), w_in_bf, ta, tb, qg, kg, bd)


def _attn_a_kernel(q_ref, kp_ref, kc_ref, kn_ref, vp_ref, vc_ref, vn_ref, o_ref,
                   m1, l1, a1, m2, l2, a2, m3, l3, a3, *, seq):
    n = pl.program_id(1)
    t = TILE_A
    w = DIL_W

    lane = lax.broadcasted_iota(I32, (1, LANES), 1)
    low = lane < HEAD_DIM
    row_i = lax.broadcasted_iota(I32, (2 * w, 1), 0) & (w - 1)
    col_m = lax.broadcasted_iota(I32, (1, 3 * w), 1)
    band_bias = jnp.where(jnp.abs(col_m - w - row_i) <= w, 0.0, NEG)
    base = n * t - t

    def window_rows(refs, start, count, d):
        parts = []
        while count > 0:
            ref = refs[start // t]
            local = start % t
            take = min(count, -(-(t - local) // d))
            parts.append(ref[pl.ds(local, take, stride=d), :] if d > 1 else ref[pl.ds(local, take), :])
            start += take * d
            count -= take
        return parts[0] if len(parts) == 1 else jnp.concatenate(parts, axis=0)

    units = []
    for d, m_s, l_s, a_s in ((1, m1, l1, a1), (4, m2, l2, a2), (16, m3, l3, a3)):
        for j in range(t // w):
            qs = (j // d) * (w * d) + (j % d)
            units.append((d, qs, m_s, l_s, a_s))

    def rows(qs, d):
        return pl.ds(qs, w, stride=d) if d > 1 else pl.ds(qs, w)

    def score(unit):
        d, qs = unit[0], unit[1]
        ks = qs + t - w * d
        q = q_ref[rows(qs, d), :]
        k = window_rows((kp_ref, kc_ref, kn_ref), ks, 3 * w, d).astype(BF16)
        q2 = jnp.concatenate([jnp.where(low, q, 0.0), jnp.where(low, 0.0, q)], axis=0).astype(BF16)
        s = lax.dot_general(q2, k, (((1,), (1,)), ((), ())), preferred_element_type=F32)
        kpos = base + ks + d * col_m
        edge_bias = jnp.where(kpos >= 0, jnp.where(kpos < seq, 0.0, NEG), NEG)
        return s + band_bias + edge_bias

    def finish(unit, s):
        d, qs, m_s, l_s, a_s = unit
        ks = qs + t - w * d
        v = window_rows((vp_ref, vc_ref, vn_ref), ks, 3 * w, d).astype(BF16)
        mx = jnp.max(s, axis=-1, keepdims=True)
        p = jnp.exp(s - mx)
        den = jnp.sum(p, axis=-1, keepdims=True)
        o = jnp.dot(p.astype(BF16), v, preferred_element_type=F32)
        m_s[rows(qs, d), :] = jnp.where(low, mx[0:w], mx[w:2 * w])
        l_s[rows(qs, d), :] = jnp.where(low, den[0:w], den[w:2 * w])
        a_s[rows(qs, d), :] = jnp.where(low, o[0:w], o[w:2 * w])

    nb = A_BATCH
    pending = [score(u) for u in units[:nb]]
    for b0 in range(0, len(units), nb):
        ahead = [score(u) for u in units[b0 + nb:b0 + 2 * nb]]
        for u, s in zip(units[b0:b0 + nb], pending):
            finish(u, s)
        pending = ahead

    mm = jnp.maximum(jnp.maximum(m1[...], m2[...]), m3[...])
    w1 = jnp.exp(m1[...] - mm)
    w2 = jnp.exp(m2[...] - mm)
    w3 = jnp.exp(m3[...] - mm)
    num = w1 * a1[...] + w2 * a2[...] + w3 * a3[...]
    den = w1 * l1[...] + w2 * l2[...] + w3 * l3[...]
    o_ref[...] = (num / den).astype(o_ref.dtype)


def _attn_a(qa, ka, va, batch, seq):
    t = TILE_A
    nt = seq // t
    q3 = qa.reshape(batch, seq, WIDTH)
    k3 = ka.reshape(batch, seq, WIDTH)
    v3 = va.reshape(batch, seq, WIDTH)
    cur = lambda b, n, p: (b, n, p)
    prev = lambda b, n, p: (b, jnp.maximum(n - 1, 0), p)
    nxt = lambda b, n, p: (b, jnp.minimum(n + 1, nt - 1), p)
    blk = (None, t, LANES)
    scr = pltpu.VMEM((t, LANES), F32)
    out = pl.pallas_call(
        functools.partial(_attn_a_kernel, seq=seq),
        grid=(batch, nt, WIDTH // LANES),
        in_specs=[pl.BlockSpec(blk, cur),
                  pl.BlockSpec(blk, prev), pl.BlockSpec(blk, cur), pl.BlockSpec(blk, nxt),
                  pl.BlockSpec(blk, prev), pl.BlockSpec(blk, cur), pl.BlockSpec(blk, nxt)],
        out_specs=pl.BlockSpec(blk, cur),
        out_shape=jax.ShapeDtypeStruct((batch, seq, WIDTH), BF16),
        scratch_shapes=[scr] * 9,
        compiler_params=_cparams(("parallel", "parallel", "parallel")),
        name="attn_dilated",
    )(q3, k3, k3, k3, v3, v3, v3)
    return out.reshape(batch * seq, WIDTH)


def _attn_b_kernel(qt_ref, k_ref, vt_ref, o_ref, m_s, acc_s, s_a, s_b, *, tq, nkv):
    m_s[...] = jnp.full(m_s.shape, NEG, F32)
    acc_s[...] = jnp.zeros(acc_s.shape, F32)
    per = COL_GROUP // tq
    ngrp = N_HEADS // per
    qg = [jnp.concatenate([qt_ref[per * g + u] for u in range(per)], axis=1) for g in range(ngrp)]
    tk = k_ref.shape[1]
    fold = 8
    ones_rows = jnp.ones((BF16_ROWS, tk), BF16)

    def score(j, dst, g):
        dst[g] = jnp.dot(k_ref[j], qg[g], preferred_element_type=F32)

    def update(j, src, g):
        cs = slice(COL_GROUP * g, COL_GROUP * (g + 1))
        r0 = HEAD_DIM * ((per * g) // (N_HEADS // 2))
        vt = jnp.concatenate([vt_ref[j, r0:r0 + HEAD_DIM, :], ones_rows], axis=0)
        s = src[g]
        part = jnp.max(s.reshape(fold, tk // fold, COL_GROUP), axis=0)
        m_old = m_s[:, cs]
        m_new = jnp.maximum(m_old, jnp.max(part, axis=0, keepdims=True))
        alpha = jnp.exp2(m_old - m_new)
        p = jnp.exp2((s - m_new).astype(BF16))
        acc_s[:, cs] = alpha * acc_s[:, cs] + jnp.dot(vt, p, preferred_element_type=F32)
        m_s[:, cs] = m_new

    def step(j_next, dst, j_cur, src):
        for g in range(ngrp):
            if j_next is not None:
                score(j_next, dst, g)
            update(j_cur, src, g)

    for g in range(ngrp):
        score(0, s_a, g)

    def body(pair, carry):
        j = 2 * pair
        step(j + 1, s_b, j, s_a)
        step(j + 2, s_a, j + 1, s_b)
        return carry

    lax.fori_loop(0, nkv // 2 - 1, body, 0)
    step(nkv - 1, s_b, nkv - 2, s_a)
    step(None, None, nkv - 1, s_b)

    o = acc_s[0:HEAD_DIM] / acc_s[HEAD_DIM:HEAD_DIM + 1]
    for c in range(N_HEADS // 2):
        pair = jnp.concatenate([o[:, (2 * c) * tq:(2 * c + 1) * tq],
                                o[:, (2 * c + 1) * tq:(2 * c + 2) * tq]], axis=0)
        o_ref[:, c * LANES:(c + 1) * LANES] = pair.T.astype(o_ref.dtype)


def _attn_b(qbt, kb, vbt, batch, seq):
    tq, tk = TQ_B, TM_PROJ
    nq = seq // tq
    nkv = seq // tk
    assert nkv % 2 == 0 and nkv >= 2
    ngrp = N_HEADS * tq // COL_GROUP
    sbuf = pltpu.VMEM((ngrp, tk, COL_GROUP), F32)
    k3 = kb.reshape(batch * nkv, tk, KV_WIDTH)
    return pl.pallas_call(
        functools.partial(_attn_b_kernel, tq=tq, nkv=nkv),
        grid=(batch, nq),
        in_specs=[pl.BlockSpec((N_HEADS, LANES, tq), lambda b, i: (0, 0, b * nq + i)),
                  pl.BlockSpec((nkv, tk, KV_WIDTH), lambda b, i: (b, 0, 0)),
                  pl.BlockSpec((nkv, KV_WIDTH, tk), lambda b, i: (b, 0, 0))],
        out_specs=pl.BlockSpec((tq, WIDTH), lambda b, i: (b * nq + i, 0)),
        out_shape=jax.ShapeDtypeStruct((batch * seq, WIDTH), BF16),
        scratch_shapes=[pltpu.VMEM((1, N_HEADS * tq), F32),
                        pltpu.VMEM((HEAD_DIM + BF16_ROWS, N_HEADS * tq), F32), sbuf, sbuf],
        compiler_params=_cparams(("parallel", "parallel")),
        name="attn_dense",
    )(qbt, k3, vbt)


def _outproj_kernel(x_ref, oa_ref, ob_ref, mod_ref, ga_ref, gb_ref, wo_ref, ln2_ref,
                    wrh_ref, wrl_ref, br_ref, ltri_ref, x1_ref, h2_ref, rt_ref, cnt_ref):
    def rms(o, g):
        return o * lax.rsqrt(jnp.mean(o * o, axis=-1, keepdims=True) + EPS) * g

    oa = rms(oa_ref[...].astype(F32), ga_ref[...])
    ob = rms(ob_ref[...].astype(F32), gb_ref[...])
    o = jnp.concatenate([oa, ob], axis=1).astype(BF16)
    mod = mod_ref[0]
    x1 = x_ref[...] + mod[2:3] * jnp.dot(o, wo_ref[...], preferred_element_type=F32)
    x1_ref[...] = x1
    inv = lax.rsqrt(jnp.mean(x1 * x1, axis=-1, keepdims=True) + EPS)
    h2 = (x1 * inv) * (ln2_ref[...] * (1.0 + mod[4:5])) + mod[3:4]
    tm = h2.shape[0]
    for c in range(ROW_TILES):
        h2_ref[pl.ds(c, tm, stride=ROW_TILES), :] = h2[:, c * LANES:(c + 1) * LANES]

    hh = h2.astype(BF16)
    hl = (h2 - hh.astype(F32)).astype(BF16)
    wrh = wrh_ref[...]
    logits = (jnp.dot(hh, wrh, preferred_element_type=F32)
              + jnp.dot(hl, wrh, preferred_element_type=F32)
              + jnp.dot(hh, wrl_ref[...], preferred_element_type=F32)) + br_ref[...]

    lane = lax.broadcasted_iota(I32, logits.shape, 1).astype(F32)
    big = jnp.float32(LANES)
    is_g = lane < N_GROUPS
    gl = jnp.where(is_g, logits, -jnp.inf)
    gmax = jnp.max(gl, axis=-1, keepdims=True)
    g_sel = jnp.min(jnp.where(gl == gmax, lane, big), axis=-1, keepdims=True)
    g_den = jnp.sum(jnp.exp(gl - gmax), axis=-1, keepdims=True)
    g_w = 1.0 / g_den
    e_lo = N_GROUPS + EXPERTS_PER_GROUP * g_sel
    is_e = (lane >= e_lo) & (lane < e_lo + EXPERTS_PER_GROUP)
    el = jnp.where(is_e, logits, -jnp.inf)
    emax = jnp.max(el, axis=-1, keepdims=True)
    p = jnp.where(is_e, jnp.exp(el - emax), -1.0)
    p1 = jnp.max(p, axis=-1, keepdims=True)
    i1 = jnp.min(jnp.where(p == p1, lane, big), axis=-1, keepdims=True)
    pm = jnp.where(lane == i1, -1.0, p)
    p2 = jnp.max(pm, axis=-1, keepdims=True)
    i2 = jnp.min(jnp.where(pm == p2, lane, big), axis=-1, keepdims=True)
    tot = p1 + p2
    gate1 = g_w * p1 / tot
    gate2 = g_w * p2 / tot
    e1 = i1 - N_GROUPS
    e2 = i2 - N_GROUPS
    memb = jnp.where(lane == e1, 1.0, jnp.where(lane == e2, 1.0, 0.0))
    before = jnp.dot(ltri_ref[...], memb.astype(BF16), preferred_element_type=F32)
    r1 = jnp.sum(jnp.where(lane == e1, before, 0.0), axis=-1, keepdims=True)
    r2 = jnp.sum(jnp.where(lane == e2, before, 0.0), axis=-1, keepdims=True)
    cnt_ref[0] = jnp.broadcast_to(jnp.sum(memb, axis=0, keepdims=True), cnt_ref.shape[1:])
    cols = (gate1, gate2, e1, e2, r1, r2)
    out = jnp.zeros_like(logits)
    for idx, col in enumerate(cols):
        out = jnp.where(lane == idx, col, out)
    rt_ref[...] = out


def _outproj(x2d, oa, ob, mod, boff, seq, ga, gb, wo_bf, ln2_g, wrh, wrl, br, ltri):
    t_tok = x2d.shape[0]
    tm = TM_PROJ
    per_seq = seq // tm
    tok = lambda i: (i, 0)
    const = lambda i: (0, 0)
    return pl.pallas_call(
        _outproj_kernel,
        grid=(t_tok // tm,),
        in_specs=[
            pl.BlockSpec((tm, D_MODEL), tok),
            pl.BlockSpec((tm, WIDTH), tok),
            pl.BlockSpec((tm, WIDTH), tok),
            pl.BlockSpec((1, 6, D_MODEL), lambda i: (boff + i // per_seq, 0, 0)),
            pl.BlockSpec((1, WIDTH), const),
            pl.BlockSpec((1, WIDTH), const),
            pl.BlockSpec((D_MODEL, D_MODEL), const),
            pl.BlockSpec((1, D_MODEL), const),
            pl.BlockSpec((D_MODEL, LANES), const),
            pl.BlockSpec((D_MODEL, LANES), const),
            pl.BlockSpec((1, LANES), const),
            pl.BlockSpec((tm, tm), const),
        ],
        out_specs=[pl.BlockSpec((tm, D_MODEL), tok),
                   pl.BlockSpec((tm * ROW_TILES, LANES), tok),
                   pl.BlockSpec((tm, LANES), tok),
                   pl.BlockSpec((1, 8, LANES), lambda i: (i, 0, 0))],
        out_shape=[jax.ShapeDtypeStruct((t_tok, D_MODEL), F32),
                   jax.ShapeDtypeStruct((t_tok * ROW_TILES, LANES), F32),
                   jax.ShapeDtypeStruct((t_tok, LANES), F32),
                   jax.ShapeDtypeStruct((t_tok // tm, 8, LANES), F32)],
        compiler_params=_cparams(("parallel",)),
        name="outproj_router",
    )(x2d, oa, ob, mod, ga.reshape(1, WIDTH), gb.reshape(1, WIDTH), wo_bf,
      ln2_g.reshape(1, D_MODEL), wrh, wrl, br, ltri)


def _permute_kernel(fill_ref, has_ref, nu_ref, idx_ref, in_hbm, out_hbm, zbuf, sem, fsem, *, dispatch):
    i = pl.program_id(0)
    n = pl.num_programs(0)
    ch = idx_ref.shape[2]
    rt = ROW_TILES
    slot = i % 2

    if dispatch:
        def fill_copy(e):
            start = pl.multiple_of(fill_ref[e] * rt, rt)
            return pltpu.make_async_copy(zbuf, out_hbm.at[pl.ds(start, zbuf.shape[0])], fsem)

        def tail_copy(b):
            start = pl.multiple_of(b * zbuf.shape[0], zbuf.shape[0])
            return pltpu.make_async_copy(zbuf, out_hbm.at[pl.ds(start, zbuf.shape[0])], fsem)

        @pl.when(i == 0)
        def _():
            zbuf[...] = jnp.zeros(zbuf.shape, F32)
            for e in range(N_EXPERTS):
                @pl.when(has_ref[e] > 0)
                def _():
                    fill_copy(e).start()
            for e in range(N_EXPERTS):
                @pl.when(has_ref[e] > 0)
                def _():
                    fill_copy(e).wait()
            n_blocks = out_hbm.shape[0] // zbuf.shape[0]

            def tail(b, carry):
                tail_copy(b).start()
                tail_copy(b).wait()
                return carry
            lax.fori_loop(nu_ref[0], n_blocks, tail, 0)

    def body(r, carry):
        a = i * ch + r
        mapped = idx_ref[0, 0, r]
        src = pl.multiple_of(((a >> 1) if dispatch else mapped) * rt, rt)
        dst = pl.multiple_of((mapped if dispatch else a) * rt, rt)
        pltpu.make_async_copy(in_hbm.at[pl.ds(src, rt)], out_hbm.at[pl.ds(dst, rt)], sem.at[slot]).start()
        return carry
    lax.fori_loop(0, ch, body, 0, unroll=8)

    def wait_step(s):
        pltpu.make_async_copy(in_hbm.at[pl.ds(0, ch * rt)], out_hbm.at[pl.ds(0, ch * rt)], sem.at[s]).wait()

    @pl.when(i > 0)
    def _():
        wait_step(1 - slot)

    @pl.when(i == n - 1)
    def _():
        wait_step(slot)


def _permute_rows(rows_in, slot_of, n_out, fill_at, has_fill, n_used, dispatch):
    ch = PERM_CHUNK
    n_steps = slot_of.shape[0] // ch
    smem_blk = pl.BlockSpec((1, 1, ch), lambda i, fa, hf, nu: (i, 0, 0), memory_space=pltpu.SMEM)
    grid_spec = pltpu.PrefetchScalarGridSpec(
        num_scalar_prefetch=3,
        grid=(n_steps,),
        in_specs=[smem_blk, pl.BlockSpec(memory_space=pl.ANY)],
        out_specs=pl.BlockSpec(memory_space=pl.ANY),
        scratch_shapes=[pltpu.VMEM((BM_MOE * ROW_TILES, LANES), F32),
                        pltpu.SemaphoreType.DMA((2,)), pltpu.SemaphoreType.DMA(())],
    )
    return pl.pallas_call(
        functools.partial(_permute_kernel, dispatch=dispatch),
        grid_spec=grid_spec,
        out_shape=jax.ShapeDtypeStruct((n_out * ROW_TILES, LANES), F32),
        compiler_params=_cparams(("arbitrary",)),
        name="moe_dispatch" if dispatch else "moe_collect",
    )(fill_at, has_fill, n_used, slot_of.reshape(n_steps, 1, ch), rows_in)


def _moe_kernel(be_ref, nu_ref, x_ref, w1_ref, w3_ref, w2_ref, y_ref):
    del be_ref
    bm = BM_MOE
    rt = ROW_TILES

    @pl.when(pl.program_id(0) < nu_ref[0])
    def _():
        x = jnp.concatenate([x_ref[pl.ds(c, bm, stride=rt), :] for c in range(rt)], axis=1).astype(BF16)
        a = jnp.dot(x, w1_ref[...], preferred_element_type=F32)
        b = jnp.dot(x, w3_ref[...], preferred_element_type=F32)
        hmid = ((a / (1.0 + jnp.exp(-a))) * b).astype(BF16)
        y = jnp.dot(hmid, w2_ref[...], preferred_element_type=F32)
        for c in range(rt):
            y_ref[pl.ds(c, bm, stride=rt), :] = y[:, c * LANES:(c + 1) * LANES]

    @pl.when(pl.program_id(0) >= nu_ref[0])
    def _():
        y_ref[...] = jnp.zeros(y_ref.shape, F32)


def _moe(xs, block_e, n_used, w1_bf, w3_bf, w2_bf):
    bm = BM_MOE
    n_blocks = block_e.shape[0]
    row_in = pl.BlockSpec((bm * ROW_TILES, LANES), lambda i, be, nu: (jnp.minimum(i, nu[0] - 1), 0))
    row_out = pl.BlockSpec((bm * ROW_TILES, LANES), lambda i, be, nu: (i, 0))
    grid_spec = pltpu.PrefetchScalarGridSpec(
        num_scalar_prefetch=2,
        grid=(n_blocks,),
        in_specs=[
            row_in,
            pl.BlockSpec((None, D_MODEL, D_EXPERT), lambda i, be, nu: (be[i], 0, 0)),
            pl.BlockSpec((None, D_MODEL, D_EXPERT), lambda i, be, nu: (be[i], 0, 0)),
            pl.BlockSpec((None, D_EXPERT, D_MODEL), lambda i, be, nu: (be[i], 0, 0)),
        ],
        out_specs=row_out,
    )
    return pl.pallas_call(
        _moe_kernel,
        grid_spec=grid_spec,
        out_shape=jax.ShapeDtypeStruct(xs.shape, F32),
        compiler_params=_cparams(("arbitrary",)),
        name="moe_experts",
    )(block_e, n_used, xs, w1_bf, w3_bf, w2_bf)


def _moe_plan(rt, cnt, tm, bm):
    t_tok = rt.shape[0]
    expert = rt[:, 2:4].astype(I32)
    rank_in_tile = rt[:, 4:6].astype(I32)
    tile_cnt = cnt[:, 0, :N_EXPERTS].astype(I32)
    tile_off = jnp.cumsum(tile_cnt, axis=0) - tile_cnt
    counts = jnp.sum(tile_cnt, axis=0)
    pcounts = (counts + bm - 1) // bm * bm
    pends = jnp.cumsum(pcounts)
    pstarts = pends - pcounts
    base = jnp.repeat(pstarts[None, :] + tile_off, tm, axis=0)
    onehot = expert[:, :, None] == jnp.arange(N_EXPERTS, dtype=I32)[None, None, :]
    dest = jnp.sum(jnp.where(onehot, base[:, None, :], 0), axis=-1) + rank_in_tile
    n_blocks = 2 * t_tok // bm + N_EXPERTS
    blk_start = jnp.arange(n_blocks, dtype=I32) * bm
    block_e = jnp.minimum(jnp.sum((pends[None, :] <= blk_start[:, None]).astype(I32), axis=1), N_EXPERTS - 1)
    n_used = (pends[-1] // bm).astype(I32).reshape(1)
    fill_at = jnp.maximum(pends - bm, 0).astype(I32)
    has_fill = (pcounts > 0).astype(I32)
    return dest.reshape(-1).astype(I32), block_e.astype(I32), n_used, fill_at, has_fill


def _final_kernel(x1_ref, y_ref, rt_ref, mod_ref, modf_ref, g_ref, o_ref):
    rt = rt_ref[...]
    tm = rt.shape[0]
    g1, g2 = rt[:, 0:1], rt[:, 1:2]
    moe = jnp.concatenate(
        [g1 * y_ref[pl.ds(c, tm, stride=2 * ROW_TILES), :]
         + g2 * y_ref[pl.ds(ROW_TILES + c, tm, stride=2 * ROW_TILES), :] for c in range(ROW_TILES)], axis=1)
    x2 = x1_ref[...] + mod_ref[0][5:6] * moe
    modf = modf_ref[0]
    inv = lax.rsqrt(jnp.mean(x2 * x2, axis=-1, keepdims=True) + EPS)
    o_ref[...] = (x2 * inv) * (g_ref[...] * (1.0 + modf[1:2])) + modf[0:1]


def _final(x1, y2, rt, mod, modf, boff, seq, lnf_g):
    t_tok = x1.shape[0]
    tm = TM_FIN
    per_seq = seq // tm
    tok = lambda i: (i, 0)
    return pl.pallas_call(
        _final_kernel,
        grid=(t_tok // tm,),
        in_specs=[pl.BlockSpec((tm, D_MODEL), tok),
                  pl.BlockSpec((tm * 2 * ROW_TILES, LANES), tok),
                  pl.BlockSpec((tm, LANES), tok),
                  pl.BlockSpec((1, 6, D_MODEL), lambda i: (boff + i // per_seq, 0, 0)),
                  pl.BlockSpec((1, 2, D_MODEL), lambda i: (boff + i // per_seq, 0, 0)),
                  pl.BlockSpec((1, D_MODEL), lambda i: (0, 0))],
        out_specs=pl.BlockSpec((tm, D_MODEL), tok),
        out_shape=jax.ShapeDtypeStruct((t_tok, D_MODEL), F32),
        compiler_params=_cparams(("parallel",)),
        name="combine_final",
    )(x1, y2, rt, mod, modf, lnf_g.reshape(1, D_MODEL))


def _trunk(x, boff, mod, modf, prm):
    batch, seq, _ = x.shape
    x2d = x.reshape(batch * seq, D_MODEL)
    ta, tb = _rope_tables(seq)
    qa, ka, va, qb8, kb, vb = _inproj(x2d, mod, boff, seq, prm["ln1_g"], prm["w_in"], ta, tb,
                                      prm["qg"], prm["kg"], prm["bd"])
    oa = _attn_a(qa, ka, va, batch, seq)
    ob = _attn_b(qb8, kb, vb, batch, seq)
    x1, h2, rt, cnt = _outproj(x2d, oa, ob, mod, boff, seq, prm["on_a_g"], prm["on_b_g"], prm["w_out"],
                               prm["ln2_g"], prm["wrh"], prm["wrl"], prm["br"], prm["ltri"])
    dest, block_e, n_used, fill_at, has_fill = _moe_plan(rt, cnt, TM_PROJ, BM_MOE)
    xs = _permute_rows(h2, dest, block_e.shape[0] * BM_MOE, fill_at, has_fill, n_used, True)
    ys = _moe(xs, block_e, n_used, prm["w1"], prm["w3"], prm["w2"])
    y2 = _permute_rows(ys, dest, dest.shape[0], fill_at, has_fill, n_used, False)
    y = _final(x1, y2, rt, mod, modf, boff, seq, prm["lnf_g"])
    return y.reshape(batch, seq, D_MODEL)


def kernel(x_prompt, x_sample, c_prompt, c_sample, ln1_g, ln2_g, w_ada, b_ada, w_in, w_out, qn_g, kn_g,
           on_a_g, on_b_g, w_rg, b_rg, w_re, b_re, w1, w3, w2, lnf_g, w_adaf, b_adaf):
    assert ln1_g.shape[0] == 1, "single-layer trunk"
    nb_p, nb_s = c_prompt.shape[0], c_sample.shape[0]
    rows = -(-(nb_p + nb_s) // 8) * 8
    c_all = jnp.concatenate([c_prompt, c_sample, jnp.zeros((rows - nb_p - nb_s, D_MODEL), F32)], axis=0)
    mod = _modulation(c_all, w_ada[0], b_ada[0]).reshape(rows, 6, D_MODEL)
    modf = _modulation(c_all, w_adaf, b_adaf).reshape(rows, 2, D_MODEL)

    wr = jnp.concatenate([w_rg[0], w_re[0], jnp.zeros((D_MODEL, LANES - N_GROUPS - N_EXPERTS), F32)], axis=1)
    wrh = wr.astype(BF16)
    wrl = (wr - wrh.astype(F32)).astype(BF16)
    br = jnp.concatenate([b_rg[0], b_re[0], jnp.zeros((LANES - N_GROUPS - N_EXPERTS,), F32)]).reshape(1, LANES)
    seg = lax.broadcasted_iota(I32, (2 * LANES, LANES), 0) % LANES // HEAD_DIM
    bd = (seg == lax.broadcasted_iota(I32, (2 * LANES, LANES), 1) // HEAD_DIM).astype(BF16)
    ltri = (lax.broadcasted_iota(I32, (TM_PROJ, TM_PROJ), 1)
            < lax.broadcasted_iota(I32, (TM_PROJ, TM_PROJ), 0)).astype(BF16)
    prm = dict(
        ltri=ltri,
        ln1_g=ln1_g[0], ln2_g=ln2_g[0], w_in=w_in[0].astype(BF16), w_out=w_out[0].astype(BF16),
        qg=jnp.tile(qn_g[0], 2).reshape(1, LANES), kg=jnp.tile(kn_g[0], 2).reshape(1, LANES), bd=bd,
        on_a_g=on_a_g[0], on_b_g=on_b_g[0], wrh=wrh, wrl=wrl, br=br,
        w1=w1[0].astype(BF16), w3=w3[0].astype(BF16), w2=w2[0].astype(BF16), lnf_g=lnf_g,
    )
    y_prompt = _trunk(x_prompt, 0, mod, modf, prm)
    y_sample = _trunk(x_sample, nb_p, mod, modf, prm)
    return (y_prompt, y_sample)
```

```python
import functools

import jax
import jax.numpy as jnp
from jax import lax
from jax.experimental import pallas as pl
from jax.experimental.pallas import tpu as pltpu

F32 = jnp.float32
BF16 = jnp.bfloat16
I32 = jnp.int32

D_MODEL = 1024
HEAD_DIM = 64
N_HEADS = 8
WIDTH = N_HEADS * HEAD_DIM
KV_WIDTH = 2 * HEAD_DIM
IN_COLS = 3 * WIDTH + WIDTH + 2 * KV_WIDTH
ROPE_THETA = 500000.0
ROPE_DIMS_A = 16
AXIAL_THETA = 10000.0
GRID_W = 64
N_GROUPS = 4
EXPERTS_PER_GROUP = 8
N_EXPERTS = 32
D_EXPERT = 512
EPS = 1e-6
NEG = -1e30
DIL_W = 64
A_BATCH = 4

LANES = 128
VMEM_LIMIT = 56 * 1024 * 1024

TM_PROJ = 512
TILE_A = 1024
TQ_B = 256
COL_GROUP = 256
BF16_ROWS = 16
LOG2E = 1.4426950408889634
ROW_TILES = D_MODEL // LANES
BM_MOE = 256
DMA_UNROLL = 8
TM_FIN = 512


def _cparams(sem):
    return pltpu.CompilerParams(dimension_semantics=sem, vmem_limit_bytes=VMEM_LIMIT)


def _mod_kernel(c_ref, w_ref, b_ref, o_ref):
    c = c_ref[...]
    a = c / (1.0 + jnp.exp(-c))
    o_ref[...] = jnp.dot(a, w_ref[...], preferred_element_type=F32,
                         precision=lax.Precision.HIGHEST) + b_ref[...]


def _modulation(c_pad, w, b):
    rows, d = c_pad.shape
    n = w.shape[1]
    tn = 1024
    return pl.pallas_call(
        _mod_kernel,
        grid=(n // tn,),
        in_specs=[pl.BlockSpec((rows, d), lambda j: (0, 0)),
                  pl.BlockSpec((d, tn), lambda j: (0, j)),
                  pl.BlockSpec((1, tn), lambda j: (0, j))],
        out_specs=pl.BlockSpec((rows, tn), lambda j: (0, j)),
        out_shape=jax.ShapeDtypeStruct((rows, n), F32),
        compiler_params=_cparams(("arbitrary",)),
        name="modulation",
    )(c_pad, w, b.reshape(1, n))


def _rope_tables(seq):
    t = jnp.arange(seq)
    tf = t.astype(F32)
    inv_a = ROPE_THETA ** (-jnp.arange(0, ROPE_DIMS_A, 2, dtype=F32) / ROPE_DIMS_A)
    ang = tf[:, None] * inv_a[None, :]
    cos, sin = jnp.cos(ang), jnp.sin(ang)
    rest = HEAD_DIM - ROPE_DIMS_A
    one = jnp.ones((seq, rest), F32)
    zero = jnp.zeros((seq, rest), F32)
    z8 = jnp.zeros_like(sin)
    cos64 = jnp.concatenate([cos, cos, one], 1)
    sm64 = jnp.concatenate([-sin, z8, zero], 1)
    sp64 = jnp.concatenate([z8, sin, zero], 1)
    ta = jnp.stack([jnp.tile(a, (1, 2)) for a in (cos64, sm64, sp64)])

    n_ax = HEAD_DIM // 2
    inv_b = AXIAL_THETA ** (-jnp.arange(0, n_ax, 2, dtype=F32) / n_ax)
    row_pos = (t // GRID_W).astype(F32)
    col_pos = (t % GRID_W).astype(F32)
    ar = row_pos[:, None] * inv_b[None, :]
    ac = col_pos[:, None] * inv_b[None, :]
    z16 = jnp.zeros_like(ar)
    cos64 = jnp.concatenate([jnp.cos(ar), jnp.cos(ar), jnp.cos(ac), jnp.cos(ac)], 1)
    sm64 = jnp.concatenate([-jnp.sin(ar), z16, -jnp.sin(ac), z16], 1)
    sp64 = jnp.concatenate([z16, jnp.sin(ar), z16, jnp.sin(ac)], 1)
    tb = jnp.stack([jnp.tile(a, (1, 2)) for a in (cos64, sm64, sp64)])
    return ta, tb


def _inproj_kernel(x_ref, mod_ref, g_ref, w_ref, ta_ref, tb_ref, qg_ref, kg_ref, bd_ref,
                   qa_ref, ka_ref, va_ref, qb_ref, kb_ref, vb_ref):
    x = x_ref[...]
    mod = mod_ref[0]
    inv = lax.rsqrt(jnp.mean(x * x, axis=-1, keepdims=True) + EPS)
    h = (x * inv) * (g_ref[...] * (1.0 + mod[1:2])) + mod[0:1]
    hb = h.astype(BF16)
    cos_a, sm_a, sp_a = ta_ref[0], ta_ref[1], ta_ref[2]
    cos_b, sm_b, sp_b = tb_ref[0], tb_ref[1], tb_ref[2]
    bd = bd_ref[...]
    lane = lax.broadcasted_iota(I32, (1, LANES), 1)
    low = lane < HEAD_DIM

    def proj(lo, n):
        return jnp.dot(hb, w_ref[:, lo:lo + n], preferred_element_type=F32)

    def rope(y, cos, sm, sp, sh):
        return y * cos + pltpu.roll(y, LANES - sh, 1) * sm + pltpu.roll(y, sh, 1) * sp

    def headnorm(y, g):
        t = y * y
        thi = t.astype(BF16)
        tlo = (t - thi.astype(F32)).astype(BF16)
        ss = jnp.dot(jnp.concatenate([thi, tlo], axis=1), bd, preferred_element_type=F32)
        return y * lax.rsqrt(ss * (1.0 / HEAD_DIM) + EPS) * g

    scale = HEAD_DIM ** -0.5
    qa = proj(0, WIDTH)
    ka = proj(WIDTH, WIDTH)
    for c in range(WIDTH // LANES):
        sl = slice(c * LANES, (c + 1) * LANES)
        qa_ref[:, sl] = rope(qa[:, sl], cos_a, sm_a, sp_a, 8) * scale
        ka_ref[:, sl] = rope(ka[:, sl], cos_a, sm_a, sp_a, 8)
    va_ref[...] = proj(2 * WIDTH, WIDTH)

    qb = proj(3 * WIDTH, WIDTH)
    qg = qg_ref[...]
    for c in range(WIDTH // LANES):
        y = rope(headnorm(qb[:, c * LANES:(c + 1) * LANES], qg), cos_b, sm_b, sp_b, 16) * (scale * LOG2E)
        ysw = pltpu.roll(y, HEAD_DIM, 1)
        grp = c // 2
        h0 = y if grp == 0 else ysw
        h1 = ysw if grp == 0 else y
        keep = low if grp == 0 else jnp.logical_not(low)
        qb_ref[2 * c] = jnp.where(keep, h0, 0.0).T.astype(BF16)
        qb_ref[2 * c + 1] = jnp.where(keep, h1, 0.0).T.astype(BF16)
    kb = proj(4 * WIDTH, KV_WIDTH)
    kb_ref[...] = rope(headnorm(kb, kg_ref[...]), cos_b, sm_b, sp_b, 16).astype(BF16)
    vb_ref[0] = proj(4 * WIDTH + KV_WIDTH, KV_WIDTH).T.astype(BF16)


def _inproj(x2d, mod, boff, seq, ln1_g, w_in_bf, ta, tb, qg, kg, bd):
    t_tok = x2d.shape[0]
    tm = TM_PROJ
    per_seq = seq // tm
    tok = lambda i: (i, 0)
    const = lambda i: (0, 0)
    return pl.pallas_call(
        _inproj_kernel,
        grid=(t_tok // tm,),
        in_specs=[
            pl.BlockSpec((tm, D_MODEL), tok),
            pl.BlockSpec((1, 6, D_MODEL), lambda i: (boff + i // per_seq, 0, 0)),
            pl.BlockSpec((1, D_MODEL), const),
            pl.BlockSpec((D_MODEL, IN_COLS), const),
            pl.BlockSpec((3, tm, LANES), lambda i: (0, i % per_seq, 0)),
            pl.BlockSpec((3, tm, LANES), lambda i: (0, i % per_seq, 0)),
            pl.BlockSpec((1, LANES), const),
            pl.BlockSpec((1, LANES), const),
            pl.BlockSpec((2 * LANES, LANES), const),
        ],
        out_specs=[
            pl.BlockSpec((tm, WIDTH), tok),
            pl.BlockSpec((tm, WIDTH), tok),
            pl.BlockSpec((tm, WIDTH), tok),
            pl.BlockSpec((N_HEADS, LANES, tm), lambda i: (0, 0, i)),
            pl.BlockSpec((tm, KV_WIDTH), tok),
            pl.BlockSpec((1, KV_WIDTH, tm), lambda i: (i, 0, 0)),
        ],
        out_shape=[
            jax.ShapeDtypeStruct((t_tok, WIDTH), F32),
            jax.ShapeDtypeStruct((t_tok, WIDTH), F32),
            jax.ShapeDtypeStruct((t_tok, WIDTH), F32),
            jax.ShapeDtypeStruct((N_HEADS, LANES, t_tok), BF16),
            jax.ShapeDtypeStruct((t_tok, KV_WIDTH), BF16),
            jax.ShapeDtypeStruct((t_tok // tm, KV_WIDTH, tm), BF16),
        ],
        compiler_params=_cparams(("parallel",)),
        name="inproj",
    )(x2d, mod, ln1_g.reshape(1, D_MODEL), w_in_bf, ta, tb, qg, kg, bd)


def _attn_a_kernel(q_ref, kp_ref, kc_ref, kn_ref, vp_ref, vc_ref, vn_ref, o_ref,
                   m1, l1, a1, m2, l2, a2, m3, l3, a3, *, seq):
    n = pl.program_id(1)
    t = TILE_A
    w = DIL_W

    lane = lax.broadcasted_iota(I32, (1, LANES), 1)
    low = lane < HEAD_DIM
    row_i = lax.broadcasted_iota(I32, (2 * w, 1), 0) & (w - 1)
    col_m = lax.broadcasted_iota(I32, (1, 3 * w), 1)
    band_bias = jnp.where(jnp.abs(col_m - w - row_i) <= w, 0.0, NEG)
    base = n * t - t

    def window_rows(refs, start, count, d):
        parts = []
        while count > 0:
            ref = refs[start // t]
            local = start % t
            take = min(count, -(-(t - local) // d))
            parts.append(ref[pl.ds(local, take, stride=d), :] if d > 1 else ref[pl.ds(local, take), :])
            start += take * d
            count -= take
        return parts[0] if len(parts) == 1 else jnp.concatenate(parts, axis=0)

    units = []
    for d, m_s, l_s, a_s in ((1, m1, l1, a1), (4, m2, l2, a2), (16, m3, l3, a3)):
        for j in range(t // w):
            qs = (j // d) * (w * d) + (j % d)
            units.append((d, qs, m_s, l_s, a_s))

    def rows(qs, d):
        return pl.ds(qs, w, stride=d) if d > 1 else pl.ds(qs, w)

    def score(unit):
        d, qs = unit[0], unit[1]
        ks = qs + t - w * d
        q = q_ref[rows(qs, d), :]
        k = window_rows((kp_ref, kc_ref, kn_ref), ks, 3 * w, d).astype(BF16)
        q2 = jnp.concatenate([jnp.where(low, q, 0.0), jnp.where(low, 0.0, q)], axis=0).astype(BF16)
        s = lax.dot_general(q2, k, (((1,), (1,)), ((), ())), preferred_element_type=F32)
        kpos = base + ks + d * col_m
        edge_bias = jnp.where(kpos >= 0, jnp.where(kpos < seq, 0.0, NEG), NEG)
        return s + band_bias + edge_bias

    def finish(unit, s):
        d, qs, m_s, l_s, a_s = unit
        ks = qs + t - w * d
        v = window_rows((vp_ref, vc_ref, vn_ref), ks, 3 * w, d).astype(BF16)
        mx = jnp.max(s, axis=-1, keepdims=True)
        p = jnp.exp(s - mx)
        den = jnp.sum(p, axis=-1, keepdims=True)
        o = jnp.dot(p.astype(BF16), v, preferred_element_type=F32)
        m_s[rows(qs, d), :] = jnp.where(low, mx[0:w], mx[w:2 * w])
        l_s[rows(qs, d), :] = jnp.where(low, den[0:w], den[w:2 * w])
        a_s[rows(qs, d), :] = jnp.where(low, o[0:w], o[w:2 * w])

    nb = A_BATCH
    pending = [score(u) for u in units[:nb]]
    for b0 in range(0, len(units), nb):
        ahead = [score(u) for u in units[b0 + nb:b0 + 2 * nb]]
        for u, s in zip(units[b0:b0 + nb], pending):
            finish(u, s)
        pending = ahead

    mm = jnp.maximum(jnp.maximum(m1[...], m2[...]), m3[...])
    w1 = jnp.exp(m1[...] - mm)
    w2 = jnp.exp(m2[...] - mm)
    w3 = jnp.exp(m3[...] - mm)
    num = w1 * a1[...] + w2 * a2[...] + w3 * a3[...]
    den = w1 * l1[...] + w2 * l2[...] + w3 * l3[...]
    o_ref[...] = (num / den).astype(o_ref.dtype)


def _attn_a(qa, ka, va, batch, seq):
    t = TILE_A
    nt = seq // t
    q3 = qa.reshape(batch, seq, WIDTH)
    k3 = ka.reshape(batch, seq, WIDTH)
    v3 = va.reshape(batch, seq, WIDTH)
    cur = lambda b, n, p: (b, n, p)
    prev = lambda b, n, p: (b, jnp.maximum(n - 1, 0), p)
    nxt = lambda b, n, p: (b, jnp.minimum(n + 1, nt - 1), p)
    blk = (None, t, LANES)
    scr = pltpu.VMEM((t, LANES), F32)
    out = pl.pallas_call(
        functools.partial(_attn_a_kernel, seq=seq),
        grid=(batch, nt, WIDTH // LANES),
        in_specs=[pl.BlockSpec(blk, cur),
                  pl.BlockSpec(blk, prev), pl.BlockSpec(blk, cur), pl.BlockSpec(blk, nxt),
                  pl.BlockSpec(blk, prev), pl.BlockSpec(blk, cur), pl.BlockSpec(blk, nxt)],
        out_specs=pl.BlockSpec(blk, cur),
        out_shape=jax.ShapeDtypeStruct((batch, seq, WIDTH), BF16),
        scratch_shapes=[scr] * 9,
        compiler_params=_cparams(("parallel", "parallel", "parallel")),
        name="attn_dilated",
    )(q3, k3, k3, k3, v3, v3, v3)
    return out.reshape(batch * seq, WIDTH)


def _attn_b_kernel(qt_ref, k_ref, vt_ref, o_ref, m_s, acc_s, s_a, s_b, *, tq, nkv):
    m_s[...] = jnp.full(m_s.shape, NEG, F32)
    acc_s[...] = jnp.zeros(acc_s.shape, F32)
    per = COL_GROUP // tq
    ngrp = N_HEADS // per
    qg = [jnp.concatenate([qt_ref[per * g + u] for u in range(per)], axis=1) for g in range(ngrp)]
    tk = k_ref.shape[1]
    fold = 8
    ones_rows = jnp.ones((BF16_ROWS, tk), BF16)

    def score(j, dst, g):
        dst[g] = jnp.dot(k_ref[j], qg[g], preferred_element_type=F32)

    def update(j, src, g):
        cs = slice(COL_GROUP * g, COL_GROUP * (g + 1))
        r0 = HEAD_DIM * ((per * g) // (N_HEADS // 2))
        vt = jnp.concatenate([vt_ref[j, r0:r0 + HEAD_DIM, :], ones_rows], axis=0)
        s = src[g]
        part = jnp.max(s.reshape(fold, tk // fold, COL_GROUP), axis=0)
        m_old = m_s[:, cs]
        m_new = jnp.maximum(m_old, jnp.max(part, axis=0, keepdims=True))
        alpha = jnp.exp2(m_old - m_new)
        p = jnp.exp2((s - m_new).astype(BF16))
        acc_s[:, cs] = alpha * acc_s[:, cs] + jnp.dot(vt, p, preferred_element_type=F32)
        m_s[:, cs] = m_new

    def step(j_next, dst, j_cur, src):
        for g in range(ngrp):
            if j_next is not None:
                score(j_next, dst, g)
            update(j_cur, src, g)

    for g in range(ngrp):
        score(0, s_a, g)

    def body(pair, carry):
        j = 2 * pair
        step(j + 1, s_b, j, s_a)
        step(j + 2, s_a, j + 1, s_b)
        return carry

    lax.fori_loop(0, nkv // 2 - 1, body, 0)
    step(nkv - 1, s_b, nkv - 2, s_a)
    step(None, None, nkv - 1, s_b)

    o = acc_s[0:HEAD_DIM] / acc_s[HEAD_DIM:HEAD_DIM + 1]
    for c in range(N_HEADS // 2):
        pair = jnp.concatenate([o[:, (2 * c) * tq:(2 * c + 1) * tq],
                                o[:, (2 * c + 1) * tq:(2 * c + 2) * tq]], axis=0)
        o_ref[:, c * LANES:(c + 1) * LANES] = pair.T.astype(o_ref.dtype)


def _attn_b(qbt, kb, vbt, batch, seq):
    tq, tk = TQ_B, TM_PROJ
    nq = seq // tq
    nkv = seq // tk
    assert nkv % 2 == 0 and nkv >= 2
    ngrp = N_HEADS * tq // COL_GROUP
    sbuf = pltpu.VMEM((ngrp, tk, COL_GROUP), F32)
    k3 = kb.reshape(batch * nkv, tk, KV_WIDTH)
    return pl.pallas_call(
        functools.partial(_attn_b_kernel, tq=tq, nkv=nkv),
        grid=(batch, nq),
        in_specs=[pl.BlockSpec((N_HEADS, LANES, tq), lambda b, i: (0, 0, b * nq + i)),
                  pl.BlockSpec((nkv, tk, KV_WIDTH), lambda b, i: (b, 0, 0)),
                  pl.BlockSpec((nkv, KV_WIDTH, tk), lambda b, i: (b, 0, 0))],
        out_specs=pl.BlockSpec((tq, WIDTH), lambda b, i: (b * nq + i, 0)),
        out_shape=jax.ShapeDtypeStruct((batch * seq, WIDTH), BF16),
        scratch_shapes=[pltpu.VMEM((1, N_HEADS * tq), F32),
                        pltpu.VMEM((HEAD_DIM + BF16_ROWS, N_HEADS * tq), F32), sbuf, sbuf],
        compiler_params=_cparams(("parallel", "parallel")),
        name="attn_dense",
    )(qbt, k3, vbt)


def _outproj_kernel(x_ref, oa_ref, ob_ref, mod_ref, ga_ref, gb_ref, wo_ref, ln2_ref,
                    wrh_ref, wrl_ref, br_ref, ltri_ref, x1_ref, h2_ref, rt_ref, cnt_ref):
    def rms(o, g):
        return o * lax.rsqrt(jnp.mean(o * o, axis=-1, keepdims=True) + EPS) * g

    oa = rms(oa_ref[...].astype(F32), ga_ref[...])
    ob = rms(ob_ref[...].astype(F32), gb_ref[...])
    o = jnp.concatenate([oa, ob], axis=1).astype(BF16)
    mod = mod_ref[0]
    x1 = x_ref[...] + mod[2:3] * jnp.dot(o, wo_ref[...], preferred_element_type=F32)
    x1_ref[...] = x1
    inv = lax.rsqrt(jnp.mean(x1 * x1, axis=-1, keepdims=True) + EPS)
    h2 = (x1 * inv) * (ln2_ref[...] * (1.0 + mod[4:5])) + mod[3:4]
    tm = h2.shape[0]
    for c in range(ROW_TILES):
        h2_ref[pl.ds(c, tm, stride=ROW_TILES), :] = h2[:, c * LANES:(c + 1) * LANES]

    hh = h2.astype(BF16)
    hl = (h2 - hh.astype(F32)).astype(BF16)
    wrh = wrh_ref[...]
    logits = (jnp.dot(hh, wrh, preferred_element_type=F32)
              + jnp.dot(hl, wrh, preferred_element_type=F32)
              + jnp.dot(hh, wrl_ref[...], preferred_element_type=F32)) + br_ref[...]

    lane = lax.broadcasted_iota(I32, logits.shape, 1).astype(F32)
    big = jnp.float32(LANES)
    is_g = lane < N_GROUPS
    gl = jnp.where(is_g, logits, -jnp.inf)
    gmax = jnp.max(gl, axis=-1, keepdims=True)
    g_sel = jnp.min(jnp.where(gl == gmax, lane, big), axis=-1, keepdims=True)
    g_den = jnp.sum(jnp.exp(gl - gmax), axis=-1, keepdims=True)
    g_w = 1.0 / g_den
    e_lo = N_GROUPS + EXPERTS_PER_GROUP * g_sel
    is_e = (lane >= e_lo) & (lane < e_lo + EXPERTS_PER_GROUP)
    el = jnp.where(is_e, logits, -jnp.inf)
    emax = jnp.max(el, axis=-1, keepdims=True)
    p = jnp.where(is_e, jnp.exp(el - emax), -1.0)
    p1 = jnp.max(p, axis=-1, keepdims=True)
    i1 = jnp.min(jnp.where(p == p1, lane, big), axis=-1, keepdims=True)
    pm = jnp.where(lane == i1, -1.0, p)
    p2 = jnp.max(pm, axis=-1, keepdims=True)
    i2 = jnp.min(jnp.where(pm == p2, lane, big), axis=-1, keepdims=True)
    tot = p1 + p2
    gate1 = g_w * p1 / tot
    gate2 = g_w * p2 / tot
    e1 = i1 - N_GROUPS
    e2 = i2 - N_GROUPS
    memb = jnp.where(lane == e1, 1.0, jnp.where(lane == e2, 1.0, 0.0))
    before = jnp.dot(ltri_ref[...], memb.astype(BF16), preferred_element_type=F32)
    r1 = jnp.sum(jnp.where(lane == e1, before, 0.0), axis=-1, keepdims=True)
    r2 = jnp.sum(jnp.where(lane == e2, before, 0.0), axis=-1, keepdims=True)
    cnt_ref[0] = jnp.broadcast_to(jnp.sum(memb, axis=0, keepdims=True), cnt_ref.shape[1:])
    cols = (gate1, gate2, e1, e2, r1, r2)
    out = jnp.zeros_like(logits)
    for idx, col in enumerate(cols):
        out = jnp.where(lane == idx, col, out)
    rt_ref[...] = out


def _outproj(x2d, oa, ob, mod, boff, seq, ga, gb, wo_bf, ln2_g, wrh, wrl, br, ltri):
    t_tok = x2d.shape[0]
    tm = TM_PROJ
    per_seq = seq // tm
    tok = lambda i: (i, 0)
    const = lambda i: (0, 0)
    return pl.pallas_call(
        _outproj_kernel,
        grid=(t_tok // tm,),
        in_specs=[
            pl.BlockSpec((tm, D_MODEL), tok),
            pl.BlockSpec((tm, WIDTH), tok),
            pl.BlockSpec((tm, WIDTH), tok),
            pl.BlockSpec((1, 6, D_MODEL), lambda i: (boff + i // per_seq, 0, 0)),
            pl.BlockSpec((1, WIDTH), const),
            pl.BlockSpec((1, WIDTH), const),
            pl.BlockSpec((D_MODEL, D_MODEL), const),
            pl.BlockSpec((1, D_MODEL), const),
            pl.BlockSpec((D_MODEL, LANES), const),
            pl.BlockSpec((D_MODEL, LANES), const),
            pl.BlockSpec((1, LANES), const),
            pl.BlockSpec((tm, tm), const),
        ],
        out_specs=[pl.BlockSpec((tm, D_MODEL), tok),
                   pl.BlockSpec((tm * ROW_TILES, LANES), tok),
                   pl.BlockSpec((tm, LANES), tok),
                   pl.BlockSpec((1, 8, LANES), lambda i: (i, 0, 0))],
        out_shape=[jax.ShapeDtypeStruct((t_tok, D_MODEL), F32),
                   jax.ShapeDtypeStruct((t_tok * ROW_TILES, LANES), F32),
                   jax.ShapeDtypeStruct((t_tok, LANES), F32),
                   jax.ShapeDtypeStruct((t_tok // tm, 8, LANES), F32)],
        compiler_params=_cparams(("parallel",)),
        name="outproj_router",
    )(x2d, oa, ob, mod, ga.reshape(1, WIDTH), gb.reshape(1, WIDTH), wo_bf,
      ln2_g.reshape(1, D_MODEL), wrh, wrl, br, ltri)


def _dispatch_kernel(fill_ref, has_ref, nu_ref, idx_ref, h2_ref, out_hbm, zbuf, sem, fsem):
    i = pl.program_id(0)
    n_asg = idx_ref.shape[2]
    rt = ROW_TILES

    def fill_copy(e):
        start = pl.multiple_of(fill_ref[e] * rt, rt)
        return pltpu.make_async_copy(zbuf, out_hbm.at[pl.ds(start, zbuf.shape[0])], fsem)

    def tail_copy(b):
        start = pl.multiple_of(b * zbuf.shape[0], zbuf.shape[0])
        return pltpu.make_async_copy(zbuf, out_hbm.at[pl.ds(start, zbuf.shape[0])], fsem)

    @pl.when(i == 0)
    def _():
        zbuf[...] = jnp.zeros(zbuf.shape, F32)
        for e in range(N_EXPERTS):
            @pl.when(has_ref[e] > 0)
            def _():
                fill_copy(e).start()
        for e in range(N_EXPERTS):
            @pl.when(has_ref[e] > 0)
            def _():
                fill_copy(e).wait()
        n_blocks = out_hbm.shape[0] // zbuf.shape[0]

        def tail(b, carry):
            tail_copy(b).start()
            tail_copy(b).wait()
            return carry
        lax.fori_loop(nu_ref[0], n_blocks, tail, 0)

    def body(r, carry):
        src = pl.multiple_of((r >> 1) * rt, rt)
        dst = pl.multiple_of(idx_ref[0, 0, r] * rt, rt)
        pltpu.make_async_copy(h2_ref.at[pl.ds(src, rt)], out_hbm.at[pl.ds(dst, rt)], sem).start()
        return carry
    lax.fori_loop(0, n_asg, body, 0, unroll=DMA_UNROLL)
    for _ in range(2):
        pltpu.make_async_copy(h2_ref, out_hbm.at[pl.ds(0, h2_ref.shape[0])], sem).wait()


def _dispatch(h2, slot_of, n_slots, fill_at, has_fill, n_used):
    tm = TM_PROJ
    n_steps = h2.shape[0] // (tm * ROW_TILES)
    grid_spec = pltpu.PrefetchScalarGridSpec(
        num_scalar_prefetch=3,
        grid=(n_steps,),
        in_specs=[pl.BlockSpec((1, 1, 2 * tm), lambda i, fa, hf, nu: (i, 0, 0), memory_space=pltpu.SMEM),
                  pl.BlockSpec((tm * ROW_TILES, LANES), lambda i, fa, hf, nu: (i, 0))],
        out_specs=pl.BlockSpec(memory_space=pl.ANY),
        scratch_shapes=[pltpu.VMEM((BM_MOE * ROW_TILES, LANES), F32),
                        pltpu.SemaphoreType.DMA(()), pltpu.SemaphoreType.DMA(())],
    )
    return pl.pallas_call(
        _dispatch_kernel,
        grid_spec=grid_spec,
        out_shape=jax.ShapeDtypeStruct((n_slots * ROW_TILES, LANES), F32),
        compiler_params=_cparams(("arbitrary",)),
        name="moe_dispatch",
    )(fill_at, has_fill, n_used, slot_of.reshape(n_steps, 1, 2 * tm), h2)


def _moe_kernel(be_ref, nu_ref, x_ref, w1_ref, w3_ref, w2_ref, y_ref):
    del be_ref
    bm = BM_MOE
    rt = ROW_TILES

    @pl.when(pl.program_id(0) < nu_ref[0])
    def _():
        x = jnp.concatenate([x_ref[pl.ds(c, bm, stride=rt), :] for c in range(rt)], axis=1).astype(BF16)
        a = jnp.dot(x, w1_ref[...], preferred_element_type=F32)
        b = jnp.dot(x, w3_ref[...], preferred_element_type=F32)
        hmid = ((a / (1.0 + jnp.exp(-a))) * b).astype(BF16)
        y = jnp.dot(hmid, w2_ref[...], preferred_element_type=F32)
        for c in range(rt):
            y_ref[pl.ds(c, bm, stride=rt), :] = y[:, c * LANES:(c + 1) * LANES]

    @pl.when(pl.program_id(0) >= nu_ref[0])
    def _():
        y_ref[...] = jnp.zeros(y_ref.shape, F32)


def _moe(xs, block_e, n_used, w1_bf, w3_bf, w2_bf):
    bm = BM_MOE
    n_blocks = block_e.shape[0]
    row_in = pl.BlockSpec((bm * ROW_TILES, LANES), lambda i, be, nu: (jnp.minimum(i, nu[0] - 1), 0))
    row_out = pl.BlockSpec((bm * ROW_TILES, LANES), lambda i, be, nu: (i, 0))
    grid_spec = pltpu.PrefetchScalarGridSpec(
        num_scalar_prefetch=2,
        grid=(n_blocks,),
        in_specs=[
            row_in,
            pl.BlockSpec((None, D_MODEL, D_EXPERT), lambda i, be, nu: (be[i], 0, 0)),
            pl.BlockSpec((None, D_MODEL, D_EXPERT), lambda i, be, nu: (be[i], 0, 0)),
            pl.BlockSpec((None, D_EXPERT, D_MODEL), lambda i, be, nu: (be[i], 0, 0)),
        ],
        out_specs=row_out,
    )
    return pl.pallas_call(
        _moe_kernel,
        grid_spec=grid_spec,
        out_shape=jax.ShapeDtypeStruct(xs.shape, F32),
        compiler_params=_cparams(("arbitrary",)),
        name="moe_experts",
    )(block_e, n_used, xs, w1_bf, w3_bf, w2_bf)


def _moe_plan(rt, cnt, tm, bm):
    t_tok = rt.shape[0]
    expert = rt[:, 2:4].astype(I32)
    rank_in_tile = rt[:, 4:6].astype(I32)
    tile_cnt = cnt[:, 0, :N_EXPERTS].astype(I32)
    tile_off = jnp.cumsum(tile_cnt, axis=0) - tile_cnt
    counts = jnp.sum(tile_cnt, axis=0)
    pcounts = (counts + bm - 1) // bm * bm
    pends = jnp.cumsum(pcounts)
    pstarts = pends - pcounts
    base = jnp.repeat(pstarts[None, :] + tile_off, tm, axis=0)
    onehot = expert[:, :, None] == jnp.arange(N_EXPERTS, dtype=I32)[None, None, :]
    dest = jnp.sum(jnp.where(onehot, base[:, None, :], 0), axis=-1) + rank_in_tile
    n_blocks = 2 * t_tok // bm + N_EXPERTS
    blk_start = jnp.arange(n_blocks, dtype=I32) * bm
    block_e = jnp.minimum(jnp.sum((pends[None, :] <= blk_start[:, None]).astype(I32), axis=1), N_EXPERTS - 1)
    n_used = (pends[-1] // bm).astype(I32).reshape(1)
    fill_at = jnp.maximum(pends - bm, 0).astype(I32)
    has_fill = (pcounts > 0).astype(I32)
    return dest.reshape(-1).astype(I32), block_e.astype(I32), n_used, fill_at, has_fill


def _final_kernel(idxc_ref, idxn_ref, x1_ref, ys_hbm, rt_ref, mod_ref, modf_ref, g_ref, o_ref, ybuf, sem):
    i = pl.program_id(0)
    n = pl.num_programs(0)
    slot = i % 2
    rt = rt_ref[...]
    tm = rt.shape[0]
    n_asg = 2 * tm
    rtl = ROW_TILES

    def start_gather(idx_ref, dst_slot):
        def body(r8, carry):
            for u in range(DMA_UNROLL):
                r = r8 * DMA_UNROLL + u
                src = pl.multiple_of(idx_ref[0, 0, r] * rtl, rtl)
                pltpu.make_async_copy(ys_hbm.at[pl.ds(src, rtl)],
                                      ybuf.at[dst_slot, pl.ds(pl.multiple_of(r * rtl, rtl), rtl)],
                                      sem.at[dst_slot]).start(priority=u % 2)
            return carry
        lax.fori_loop(0, n_asg // DMA_UNROLL, body, 0)

    @pl.when(i == 0)
    def _():
        start_gather(idxc_ref, 0)

    @pl.when(i + 1 < n)
    def _():
        start_gather(idxn_ref, 1 - slot)

    pltpu.make_async_copy(ys_hbm.at[pl.ds(0, n_asg * rtl)], ybuf.at[slot], sem.at[slot]).wait()

    g1, g2 = rt[:, 0:1], rt[:, 1:2]
    moe = jnp.concatenate(
        [g1 * ybuf[slot, pl.ds(c, tm, stride=2 * rtl), :]
         + g2 * ybuf[slot, pl.ds(rtl + c, tm, stride=2 * rtl), :] for c in range(rtl)], axis=1)
    x2 = x1_ref[...] + mod_ref[0][5:6] * moe
    modf = modf_ref[0]
    inv = lax.rsqrt(jnp.mean(x2 * x2, axis=-1, keepdims=True) + EPS)
    o_ref[...] = (x2 * inv) * (g_ref[...] * (1.0 + modf[1:2])) + modf[0:1]


def _final(x1, ys, slot_of, rt, mod, modf, boff, seq, lnf_g):
    t_tok = x1.shape[0]
    tm = TM_FIN
    n_steps = t_tok // tm
    per_seq = seq // tm
    tok = lambda i: (i, 0)
    idx3 = slot_of.reshape(n_steps, 1, 2 * tm)
    smem_blk = lambda fn: pl.BlockSpec((1, 1, 2 * tm), fn, memory_space=pltpu.SMEM)
    return pl.pallas_call(
        _final_kernel,
        grid=(n_steps,),
        in_specs=[smem_blk(lambda i: (i, 0, 0)),
                  smem_blk(lambda i: (jnp.minimum(i + 1, n_steps - 1), 0, 0)),
                  pl.BlockSpec((tm, D_MODEL), tok),
                  pl.BlockSpec(memory_space=pl.ANY),
                  pl.BlockSpec((tm, LANES), tok),
                  pl.BlockSpec((1, 6, D_MODEL), lambda i: (boff + i // per_seq, 0, 0)),
                  pl.BlockSpec((1, 2, D_MODEL), lambda i: (boff + i // per_seq, 0, 0)),
                  pl.BlockSpec((1, D_MODEL), lambda i: (0, 0))],
        out_specs=pl.BlockSpec((tm, D_MODEL), tok),
        out_shape=jax.ShapeDtypeStruct((t_tok, D_MODEL), F32),
        scratch_shapes=[pltpu.VMEM((2, tm * 2 * ROW_TILES, LANES), F32), pltpu.SemaphoreType.DMA((2,))],
        compiler_params=_cparams(("arbitrary",)),
        name="combine_final",
    )(idx3, idx3, x1, ys, rt, mod, modf, lnf_g.reshape(1, D_MODEL))


def _trunk(x, boff, mod, modf, prm):
    batch, seq, _ = x.shape
    x2d = x.reshape(batch * seq, D_MODEL)
    ta, tb = _rope_tables(seq)
    qa, ka, va, qb8, kb, vb = _inproj(x2d, mod, boff, seq, prm["ln1_g"], prm["w_in"], ta, tb,
                                      prm["qg"], prm["kg"], prm["bd"])
    oa = _attn_a(qa, ka, va, batch, seq)
    ob = _attn_b(qb8, kb, vb, batch, seq)
    x1, h2, rt, cnt = _outproj(x2d, oa, ob, mod, boff, seq, prm["on_a_g"], prm["on_b_g"], prm["w_out"],
                               prm["ln2_g"], prm["wrh"], prm["wrl"], prm["br"], prm["ltri"])
    dest, block_e, n_used, fill_at, has_fill = _moe_plan(rt, cnt, TM_PROJ, BM_MOE)
    xs = _dispatch(h2, dest, block_e.shape[0] * BM_MOE, fill_at, has_fill, n_used)
    ys = _moe(xs, block_e, n_used, prm["w1"], prm["w3"], prm["w2"])
    y = _final(x1, ys, dest, rt, mod, modf, boff, seq, prm["lnf_g"])
    return y.reshape(batch, seq, D_MODEL)


def kernel(x_prompt, x_sample, c_prompt, c_sample, ln1_g, ln2_g, w_ada, b_ada, w_in, w_out, qn_g, kn_g,
           on_a_g, on_b_g, w_rg, b_rg, w_re, b_re, w1, w3, w2, lnf_g, w_adaf, b_adaf):
    assert ln1_g.shape[0] == 1, "single-layer trunk"
    nb_p, nb_s = c_prompt.shape[0], c_sample.shape[0]
    rows = -(-(nb_p + nb_s) // 8) * 8
    c_all = jnp.concatenate([c_prompt, c_sample, jnp.zeros((rows - nb_p - nb_s, D_MODEL), F32)], axis=0)
    mod = _modulation(c_all, w_ada[0], b_ada[0]).reshape(rows, 6, D_MODEL)
    modf = _modulation(c_all, w_adaf, b_adaf).reshape(rows, 2, D_MODEL)

    wr = jnp.concatenate([w_rg[0], w_re[0], jnp.zeros((D_MODEL, LANES - N_GROUPS - N_EXPERTS), F32)], axis=1)
    wrh = wr.astype(BF16)
    wrl = (wr - wrh.astype(F32)).astype(BF16)
    br = jnp.concatenate([b_rg[0], b_re[0], jnp.zeros((LANES - N_GROUPS - N_EXPERTS,), F32)]).reshape(1, LANES)
    seg = lax.broadcasted_iota(I32, (2 * LANES, LANES), 0) % LANES // HEAD_DIM
    bd = (seg == lax.broadcasted_iota(I32, (2 * LANES, LANES), 1) // HEAD_DIM).astype(BF16)
    ltri = (lax.broadcasted_iota(I32, (TM_PROJ, TM_PROJ), 1)
            < lax.broadcasted_iota(I32, (TM_PROJ, TM_PROJ), 0)).astype(BF16)
    prm = dict(
        ltri=ltri,
        ln1_g=ln1_g[0], ln2_g=ln2_g[0], w_in=w_in[0].astype(BF16), w_out=w_out[0].astype(BF16),
        qg=jnp.tile(qn_g[0], 2).reshape(1, LANES), kg=jnp.tile(kn_g[0], 2).reshape(1, LANES), bd=bd,
        on_a_g=on_a_g[0], on_b_g=on_b_g[0], wrh=wrh, wrl=wrl, br=br,
        w1=w1[0].astype(BF16), w3=w3[0].astype(BF16), w2=w2[0].astype(BF16), lnf_g=lnf_g,
    )
    y_prompt = _trunk(x_prompt, 0, mod, modf, prm)
    y_sample = _trunk(x_sample, nb_p, mod, modf, prm)
    return (y_prompt, y_sample)
```

```python
import functools

import jax
import jax.numpy as jnp
from jax import lax
from jax.experimental import pallas as pl
from jax.experimental.pallas import tpu as pltpu

F32 = jnp.float32
BF16 = jnp.bfloat16
I32 = jnp.int32

D_MODEL = 1024
HEAD_DIM = 64
N_HEADS = 8
WIDTH = N_HEADS * HEAD_DIM
KV_WIDTH = 2 * HEAD_DIM
IN_COLS = 3 * WIDTH + WIDTH + 2 * KV_WIDTH
ROPE_THETA = 500000.0
ROPE_DIMS_A = 16
AXIAL_THETA = 10000.0
GRID_W = 64
N_GROUPS = 4
EXPERTS_PER_GROUP = 8
N_EXPERTS = 32
D_EXPERT = 512
EPS = 1e-6
NEG = -1e30
DIL_W = 64
A_BATCH = 8

LANES = 128
VMEM_LIMIT = 56 * 1024 * 1024

TM_PROJ = 512
TILE_A = 1024
TQ_B = 256
NQ_B = 4
B_ITEMS_PER_TRIP = 8
BF16_ROWS = 16
LOG2E = 1.4426950408889634
ROW_TILES = D_MODEL // LANES
BM_MOE = 256
DMA_UNROLL = 8
TM_FIN = 512


def _cparams(sem):
    return pltpu.CompilerParams(dimension_semantics=sem, vmem_limit_bytes=VMEM_LIMIT)


def _mod_kernel(c_ref, w_ref, b_ref, o_ref):
    c = c_ref[...]
    a = c / (1.0 + jnp.exp(-c))
    o_ref[...] = jnp.dot(a, w_ref[...], preferred_element_type=F32,
                         precision=lax.Precision.HIGHEST) + b_ref[...]


def _modulation(c_pad, w, b):
    rows, d = c_pad.shape
    n = w.shape[1]
    tn = 1024
    return pl.pallas_call(
        _mod_kernel,
        grid=(n // tn,),
        in_specs=[pl.BlockSpec((rows, d), lambda j: (0, 0)),
                  pl.BlockSpec((d, tn), lambda j: (0, j)),
                  pl.BlockSpec((1, tn), lambda j: (0, j))],
        out_specs=pl.BlockSpec((rows, tn), lambda j: (0, j)),
        out_shape=jax.ShapeDtypeStruct((rows, n), F32),
        compiler_params=_cparams(("arbitrary",)),
        name="modulation",
    )(c_pad, w, b.reshape(1, n))


def _rope_tables(seq):
    t = jnp.arange(seq)
    tf = t.astype(F32)
    inv_a = ROPE_THETA ** (-jnp.arange(0, ROPE_DIMS_A, 2, dtype=F32) / ROPE_DIMS_A)
    ang = tf[:, None] * inv_a[None, :]
    cos, sin = jnp.cos(ang), jnp.sin(ang)
    rest = HEAD_DIM - ROPE_DIMS_A
    one = jnp.ones((seq, rest), F32)
    zero = jnp.zeros((seq, rest), F32)
    z8 = jnp.zeros_like(sin)
    cos64 = jnp.concatenate([cos, cos, one], 1)
    sm64 = jnp.concatenate([-sin, z8, zero], 1)
    sp64 = jnp.concatenate([z8, sin, zero], 1)
    ta = jnp.stack([jnp.tile(a, (1, 2)) for a in (cos64, sm64, sp64)])

    n_ax = HEAD_DIM // 2
    inv_b = AXIAL_THETA ** (-jnp.arange(0, n_ax, 2, dtype=F32) / n_ax)
    row_pos = (t // GRID_W).astype(F32)
    col_pos = (t % GRID_W).astype(F32)
    ar = row_pos[:, None] * inv_b[None, :]
    ac = col_pos[:, None] * inv_b[None, :]
    z16 = jnp.zeros_like(ar)
    cos64 = jnp.concatenate([jnp.cos(ar), jnp.cos(ar), jnp.cos(ac), jnp.cos(ac)], 1)
    sm64 = jnp.concatenate([-jnp.sin(ar), z16, -jnp.sin(ac), z16], 1)
    sp64 = jnp.concatenate([z16, jnp.sin(ar), z16, jnp.sin(ac)], 1)
    tb = jnp.stack([jnp.tile(a, (1, 2)) for a in (cos64, sm64, sp64)])
    return ta, tb


def _inproj_kernel(x_ref, mod_ref, g_ref, w_ref, ta_ref, tb_ref, qg_ref, kg_ref, bd_ref,
                   qa_ref, ka_ref, va_ref, qb_ref, kb_ref, vb_ref):
    x = x_ref[...]
    mod = mod_ref[0]
    inv = lax.rsqrt(jnp.mean(x * x, axis=-1, keepdims=True) + EPS)
    h = (x * inv) * (g_ref[...] * (1.0 + mod[1:2])) + mod[0:1]
    hb = h.astype(BF16)
    cos_a, sm_a, sp_a = ta_ref[0], ta_ref[1], ta_ref[2]
    cos_b, sm_b, sp_b = tb_ref[0], tb_ref[1], tb_ref[2]
    bd = bd_ref[...]
    lane = lax.broadcasted_iota(I32, (1, LANES), 1)
    low = lane < HEAD_DIM

    def proj(lo, n):
        return jnp.dot(hb, w_ref[:, lo:lo + n], preferred_element_type=F32)

    def rope(y, cos, sm, sp, sh):
        return y * cos + pltpu.roll(y, LANES - sh, 1) * sm + pltpu.roll(y, sh, 1) * sp

    def headnorm(y, g):
        t = y * y
        thi = t.astype(BF16)
        tlo = (t - thi.astype(F32)).astype(BF16)
        ss = jnp.dot(jnp.concatenate([thi, tlo], axis=1), bd, preferred_element_type=F32)
        return y * lax.rsqrt(ss * (1.0 / HEAD_DIM) + EPS) * g

    scale = HEAD_DIM ** -0.5
    qa = proj(0, WIDTH)
    ka = proj(WIDTH, WIDTH)
    for c in range(WIDTH // LANES):
        sl = slice(c * LANES, (c + 1) * LANES)
        qa_ref[:, sl] = rope(qa[:, sl], cos_a, sm_a, sp_a, 8) * scale
        ka_ref[:, sl] = rope(ka[:, sl], cos_a, sm_a, sp_a, 8)
    va_ref[...] = proj(2 * WIDTH, WIDTH)

    qb = proj(3 * WIDTH, WIDTH)
    qg = qg_ref[...]
    for c in range(WIDTH // LANES):
        y = rope(headnorm(qb[:, c * LANES:(c + 1) * LANES], qg), cos_b, sm_b, sp_b, 16) * (scale * LOG2E)
        ysw = pltpu.roll(y, HEAD_DIM, 1)
        grp = c // 2
        h0 = y if grp == 0 else ysw
        h1 = ysw if grp == 0 else y
        keep = low if grp == 0 else jnp.logical_not(low)
        for u, hh in enumerate((h0, h1)):
            ht = jnp.where(keep, hh, 0.0).T.astype(BF16)
            for sub in range(ht.shape[1] // TQ_B):
                qb_ref[sub, 2 * c + u] = ht[:, sub * TQ_B:(sub + 1) * TQ_B]
    kb = proj(4 * WIDTH, KV_WIDTH)
    kb_ref[...] = rope(headnorm(kb, kg_ref[...]), cos_b, sm_b, sp_b, 16).astype(BF16)
    vb_ref[0] = proj(4 * WIDTH + KV_WIDTH, KV_WIDTH).T.astype(BF16)


def _inproj(x2d, mod, boff, seq, ln1_g, w_in_bf, ta, tb, qg, kg, bd):
    t_tok = x2d.shape[0]
    tm = TM_PROJ
    per_seq = seq // tm
    tok = lambda i: (i, 0)
    const = lambda i: (0, 0)
    return pl.pallas_call(
        _inproj_kernel,
        grid=(t_tok // tm,),
        in_specs=[
            pl.BlockSpec((tm, D_MODEL), tok),
            pl.BlockSpec((1, 6, D_MODEL), lambda i: (boff + i // per_seq, 0, 0)),
            pl.BlockSpec((1, D_MODEL), const),
            pl.BlockSpec((D_MODEL, IN_COLS), const),
            pl.BlockSpec((3, tm, LANES), lambda i: (0, i % per_seq, 0)),
            pl.BlockSpec((3, tm, LANES), lambda i: (0, i % per_seq, 0)),
            pl.BlockSpec((1, LANES), const),
            pl.BlockSpec((1, LANES), const),
            pl.BlockSpec((2 * LANES, LANES), const),
        ],
        out_specs=[
            pl.BlockSpec((tm, WIDTH), tok),
            pl.BlockSpec((tm, WIDTH), tok),
            pl.BlockSpec((tm, WIDTH), tok),
            pl.BlockSpec((tm // TQ_B, N_HEADS, LANES, TQ_B), lambda i: (i, 0, 0, 0)),
            pl.BlockSpec((tm, KV_WIDTH), tok),
            pl.BlockSpec((1, KV_WIDTH, tm), lambda i: (i, 0, 0)),
        ],
        out_shape=[
            jax.ShapeDtypeStruct((t_tok, WIDTH), F32),
            jax.ShapeDtypeStruct((t_tok, WIDTH), F32),
            jax.ShapeDtypeStruct((t_tok, WIDTH), F32),
            jax.ShapeDtypeStruct((t_tok // TQ_B, N_HEADS, LANES, TQ_B), BF16),
            jax.ShapeDtypeStruct((t_tok, KV_WIDTH), BF16),
            jax.ShapeDtypeStruct((t_tok // tm, KV_WIDTH, tm), BF16),
        ],
        compiler_params=_cparams(("parallel",)),
        name="inproj",
    )(x2d, mod, ln1_g.reshape(1, D_MODEL), w_in_bf, ta, tb, qg, kg, bd)


def _attn_a_kernel(q_ref, kp_ref, kc_ref, kn_ref, vp_ref, vc_ref, vn_ref, o_ref,
                   m1, l1, a1, m2, l2, a2, m3, l3, a3, *, seq):
    n = pl.program_id(1)
    t = TILE_A
    w = DIL_W

    lane = lax.broadcasted_iota(I32, (1, LANES), 1)
    low = lane < HEAD_DIM
    row_i = lax.broadcasted_iota(I32, (2 * w, 1), 0) & (w - 1)
    col_m = lax.broadcasted_iota(I32, (1, 3 * w), 1)
    band_bias = jnp.where(jnp.abs(col_m - w - row_i) <= w, 0.0, NEG)
    base = n * t - t

    def window_rows(refs, start, count, d):
        parts = []
        while count > 0:
            ref = refs[start // t]
            local = start % t
            take = min(count, -(-(t - local) // d))
            parts.append(ref[pl.ds(local, take, stride=d), :] if d > 1 else ref[pl.ds(local, take), :])
            start += take * d
            count -= take
        return parts[0] if len(parts) == 1 else jnp.concatenate(parts, axis=0)

    units = []
    for d, m_s, l_s, a_s in ((1, m1, l1, a1), (4, m2, l2, a2), (16, m3, l3, a3)):
        for j in range(t // w):
            qs = (j // d) * (w * d) + (j % d)
            units.append((d, qs, m_s, l_s, a_s))

    def rows(qs, d):
        return pl.ds(qs, w, stride=d) if d > 1 else pl.ds(qs, w)

    def score(unit):
        d, qs = unit[0], unit[1]
        ks = qs + t - w * d
        q = q_ref[rows(qs, d), :]
        k = window_rows((kp_ref, kc_ref, kn_ref), ks, 3 * w, d).astype(BF16)
        q2 = jnp.concatenate([jnp.where(low, q, 0.0), jnp.where(low, 0.0, q)], axis=0).astype(BF16)
        s = lax.dot_general(q2, k, (((1,), (1,)), ((), ())), preferred_element_type=F32)
        kpos = base + ks + d * col_m
        edge_bias = jnp.where(kpos >= 0, jnp.where(kpos < seq, 0.0, NEG), NEG)
        return s + band_bias + edge_bias

    def finish(unit, s):
        d, qs, m_s, l_s, a_s = unit
        ks = qs + t - w * d
        v = window_rows((vp_ref, vc_ref, vn_ref), ks, 3 * w, d).astype(BF16)
        mx = jnp.max(s, axis=-1, keepdims=True)
        p = jnp.exp(s - mx)
        den = jnp.sum(p, axis=-1, keepdims=True)
        o = jnp.dot(p.astype(BF16), v, preferred_element_type=F32)
        m_s[rows(qs, d), :] = jnp.where(low, mx[0:w], mx[w:2 * w])
        l_s[rows(qs, d), :] = jnp.where(low, den[0:w], den[w:2 * w])
        a_s[rows(qs, d), :] = jnp.where(low, o[0:w], o[w:2 * w])

    nb = A_BATCH
    pending = [score(u) for u in units[:nb]]
    for b0 in range(0, len(units), nb):
        ahead = [score(u) for u in units[b0 + nb:b0 + 2 * nb]]
        for u, s in zip(units[b0:b0 + nb], pending):
            finish(u, s)
        pending = ahead

    mm = jnp.maximum(jnp.maximum(m1[...], m2[...]), m3[...])
    w1 = jnp.exp(m1[...] - mm)
    w2 = jnp.exp(m2[...] - mm)
    w3 = jnp.exp(m3[...] - mm)
    num = w1 * a1[...] + w2 * a2[...] + w3 * a3[...]
    den = w1 * l1[...] + w2 * l2[...] + w3 * l3[...]
    o_ref[...] = (num / den).astype(o_ref.dtype)


def _attn_a(qa, ka, va, batch, seq):
    t = TILE_A
    nt = seq // t
    q3 = qa.reshape(batch, seq, WIDTH)
    k3 = ka.reshape(batch, seq, WIDTH)
    v3 = va.reshape(batch, seq, WIDTH)
    cur = lambda b, n, p: (b, n, p)
    prev = lambda b, n, p: (b, jnp.maximum(n - 1, 0), p)
    nxt = lambda b, n, p: (b, jnp.minimum(n + 1, nt - 1), p)
    blk = (None, t, LANES)
    scr = pltpu.VMEM((t, LANES), F32)
    out = pl.pallas_call(
        functools.partial(_attn_a_kernel, seq=seq),
        grid=(batch, nt, WIDTH // LANES),
        in_specs=[pl.BlockSpec(blk, cur),
                  pl.BlockSpec(blk, prev), pl.BlockSpec(blk, cur), pl.BlockSpec(blk, nxt),
                  pl.BlockSpec(blk, prev), pl.BlockSpec(blk, cur), pl.BlockSpec(blk, nxt)],
        out_specs=pl.BlockSpec(blk, cur),
        out_shape=jax.ShapeDtypeStruct((batch, seq, WIDTH), BF16),
        scratch_shapes=[scr] * 9,
        compiler_params=_cparams(("parallel", "parallel", "parallel")),
        name="attn_dilated",
    )(q3, k3, k3, k3, v3, v3, v3)
    return out.reshape(batch * seq, WIDTH)


def _attn_b_kernel(qt_ref, k_ref, vt_ref, o_ref, m_s, acc_s, s_a, s_b, *, nkv):
    nqt, tq = qt_ref.shape[0], qt_ref.shape[3]
    tk = k_ref.shape[1]
    m_s[...] = jnp.full(m_s.shape, NEG, F32)
    acc_s[...] = jnp.zeros(acc_s.shape, F32)
    fold = 8
    ones_rows = jnp.ones((BF16_ROWS, tk), BF16)
    log_nkv = nkv.bit_length() - 1

    def score(t, dst, h):
        dst[h] = jnp.dot(k_ref[t & (nkv - 1)], qt_ref[t >> log_nkv, h], preferred_element_type=F32)

    def update(t, src, h):
        qi, j = t >> log_nkv, t & (nkv - 1)
        cs = slice(tq * h, tq * (h + 1))
        r0 = HEAD_DIM * (h // (N_HEADS // 2))
        vt = jnp.concatenate([vt_ref[j, r0:r0 + HEAD_DIM, :], ones_rows], axis=0)
        s = src[h]
        part = jnp.max(s.reshape(fold, tk // fold, tq), axis=0)
        m_old = m_s[qi, :, cs]
        m_new = jnp.maximum(m_old, jnp.max(part, axis=0, keepdims=True))
        alpha = jnp.exp2(m_old - m_new)
        p = jnp.exp2((s - m_new).astype(BF16))
        acc_s[qi, :, cs] = alpha * acc_s[qi, :, cs] + jnp.dot(vt, p, preferred_element_type=F32)
        m_s[qi, :, cs] = m_new

    def step(t_next, dst, t_cur, src):
        for h in range(N_HEADS):
            if t_next is not None:
                score(t_next, dst, h)
            update(t_cur, src, h)

    n_items = nqt * nkv
    for h in range(N_HEADS):
        score(0, s_a, h)

    bufs = (s_a, s_b)
    per_trip = B_ITEMS_PER_TRIP

    def body(trip, carry):
        t0 = per_trip * trip
        for u in range(per_trip):
            step(t0 + u + 1, bufs[(u + 1) % 2], t0 + u, bufs[u % 2])
        return carry

    lax.fori_loop(0, n_items // per_trip - 1, body, 0)
    t0 = n_items - per_trip
    for u in range(per_trip - 1):
        step(t0 + u + 1, bufs[(u + 1) % 2], t0 + u, bufs[u % 2])
    step(None, None, n_items - 1, bufs[(per_trip - 1) % 2])

    for qi in range(nqt):
        o = acc_s[qi, 0:HEAD_DIM] / acc_s[qi, HEAD_DIM:HEAD_DIM + 1]
        for c in range(N_HEADS // 2):
            pair = jnp.concatenate([o[:, (2 * c) * tq:(2 * c + 1) * tq],
                                    o[:, (2 * c + 1) * tq:(2 * c + 2) * tq]], axis=0)
            o_ref[qi * tq:(qi + 1) * tq, c * LANES:(c + 1) * LANES] = pair.T.astype(o_ref.dtype)


def _attn_b(qbt, kb, vbt, batch, seq):
    tq, tk, nqt = TQ_B, TM_PROJ, NQ_B
    steps = seq // (tq * nqt)
    nkv = seq // tk
    assert nkv & (nkv - 1) == 0 and nkv >= 2
    sbuf = pltpu.VMEM((N_HEADS, tk, tq), F32)
    k3 = kb.reshape(batch * nkv, tk, KV_WIDTH)
    return pl.pallas_call(
        functools.partial(_attn_b_kernel, nkv=nkv),
        grid=(batch, steps),
        in_specs=[pl.BlockSpec((nqt, N_HEADS, LANES, tq), lambda b, i: (b * steps + i, 0, 0, 0)),
                  pl.BlockSpec((nkv, tk, KV_WIDTH), lambda b, i: (b, 0, 0)),
                  pl.BlockSpec((nkv, KV_WIDTH, tk), lambda b, i: (b, 0, 0))],
        out_specs=pl.BlockSpec((nqt * tq, WIDTH), lambda b, i: (b * steps + i, 0)),
        out_shape=jax.ShapeDtypeStruct((batch * seq, WIDTH), BF16),
        scratch_shapes=[pltpu.VMEM((nqt, 1, N_HEADS * tq), F32),
                        pltpu.VMEM((nqt, HEAD_DIM + BF16_ROWS, N_HEADS * tq), F32), sbuf, sbuf],
        compiler_params=_cparams(("parallel", "parallel")),
        name="attn_dense",
    )(qbt, k3, vbt)


def _outproj_kernel(x_ref, oa_ref, ob_ref, mod_ref, ga_ref, gb_ref, wo_ref, ln2_ref,
                    wrh_ref, wrl_ref, br_ref, ltri_ref, x1_ref, h2_ref, rt_ref, cnt_ref):
    def rms(o, g):
        return o * lax.rsqrt(jnp.mean(o * o, axis=-1, keepdims=True) + EPS) * g

    oa = rms(oa_ref[...].astype(F32), ga_ref[...])
    ob = rms(ob_ref[...].astype(F32), gb_ref[...])
    o = jnp.concatenate([oa, ob], axis=1).astype(BF16)
    mod = mod_ref[0]
    x1 = x_ref[...] + mod[2:3] * jnp.dot(o, wo_ref[...], preferred_element_type=F32)
    x1_ref[...] = x1
    inv = lax.rsqrt(jnp.mean(x1 * x1, axis=-1, keepdims=True) + EPS)
    h2 = (x1 * inv) * (ln2_ref[...] * (1.0 + mod[4:5])) + mod[3:4]
    tm = h2.shape[0]
    for c in range(ROW_TILES):
        h2_ref[pl.ds(c, tm, stride=ROW_TILES), :] = h2[:, c * LANES:(c + 1) * LANES]

    hh = h2.astype(BF16)
    hl = (h2 - hh.astype(F32)).astype(BF16)
    wrh = wrh_ref[...]
    logits = (jnp.dot(hh, wrh, preferred_element_type=F32)
              + jnp.dot(hl, wrh, preferred_element_type=F32)
              + jnp.dot(hh, wrl_ref[...], preferred_element_type=F32)) + br_ref[...]

    lane = lax.broadcasted_iota(I32, logits.shape, 1).astype(F32)
    big = jnp.float32(LANES)
    is_g = lane < N_GROUPS
    gl = jnp.where(is_g, logits, -jnp.inf)
    gmax = jnp.max(gl, axis=-1, keepdims=True)
    g_sel = jnp.min(jnp.where(gl == gmax, lane, big), axis=-1, keepdims=True)
    g_den = jnp.sum(jnp.exp(gl - gmax), axis=-1, keepdims=True)
    g_w = 1.0 / g_den
    e_lo = N_GROUPS + EXPERTS_PER_GROUP * g_sel
    is_e = (lane >= e_lo) & (lane < e_lo + EXPERTS_PER_GROUP)
    el = jnp.where(is_e, logits, -jnp.inf)
    emax = jnp.max(el, axis=-1, keepdims=True)
    p = jnp.where(is_e, jnp.exp(el - emax), -1.0)
    p1 = jnp.max(p, axis=-1, keepdims=True)
    i1 = jnp.min(jnp.where(p == p1, lane, big), axis=-1, keepdims=True)
    pm = jnp.where(lane == i1, -1.0, p)
    p2 = jnp.max(pm, axis=-1, keepdims=True)
    i2 = jnp.min(jnp.where(pm == p2, lane, big), axis=-1, keepdims=True)
    tot = p1 + p2
    gate1 = g_w * p1 / tot
    gate2 = g_w * p2 / tot
    e1 = i1 - N_GROUPS
    e2 = i2 - N_GROUPS
    memb = jnp.where(lane == e1, 1.0, jnp.where(lane == e2, 1.0, 0.0))
    before = jnp.dot(ltri_ref[...], memb.astype(BF16), preferred_element_type=F32)
    r1 = jnp.sum(jnp.where(lane == e1, before, 0.0), axis=-1, keepdims=True)
    r2 = jnp.sum(jnp.where(lane == e2, before, 0.0), axis=-1, keepdims=True)
    cnt_ref[0] = jnp.broadcast_to(jnp.sum(memb, axis=0, keepdims=True), cnt_ref.shape[1:])
    cols = (gate1, gate2, e1, e2, r1, r2)
    out = jnp.zeros_like(logits)
    for idx, col in enumerate(cols):
        out = jnp.where(lane == idx, col, out)
    rt_ref[...] = out


def _outproj(x2d, oa, ob, mod, boff, seq, ga, gb, wo_bf, ln2_g, wrh, wrl, br, ltri):
    t_tok = x2d.shape[0]
    tm = TM_PROJ
    per_seq = seq // tm
    tok = lambda i: (i, 0)
    const = lambda i: (0, 0)
    return pl.pallas_call(
        _outproj_kernel,
        grid=(t_tok // tm,),
        in_specs=[
            pl.BlockSpec((tm, D_MODEL), tok),
            pl.BlockSpec((tm, WIDTH), tok),
            pl.BlockSpec((tm, WIDTH), tok),
            pl.BlockSpec((1, 6, D_MODEL), lambda i: (boff + i // per_seq, 0, 0)),
            pl.BlockSpec((1, WIDTH), const),
            pl.BlockSpec((1, WIDTH), const),
            pl.BlockSpec((D_MODEL, D_MODEL), const),
            pl.BlockSpec((1, D_MODEL), const),
            pl.BlockSpec((D_MODEL, LANES), const),
            pl.BlockSpec((D_MODEL, LANES), const),
            pl.BlockSpec((1, LANES), const),
            pl.BlockSpec((tm, tm), const),
        ],
        out_specs=[pl.BlockSpec((tm, D_MODEL), tok),
                   pl.BlockSpec((tm * ROW_TILES, LANES), tok),
                   pl.BlockSpec((tm, LANES), tok),
                   pl.BlockSpec((1, 8, LANES), lambda i: (i, 0, 0))],
        out_shape=[jax.ShapeDtypeStruct((t_tok, D_MODEL), F32),
                   jax.ShapeDtypeStruct((t_tok * ROW_TILES, LANES), F32),
                   jax.ShapeDtypeStruct((t_tok, LANES), F32),
                   jax.ShapeDtypeStruct((t_tok // tm, 8, LANES), F32)],
        compiler_params=_cparams(("parallel",)),
        name="outproj_router",
    )(x2d, oa, ob, mod, ga.reshape(1, WIDTH), gb.reshape(1, WIDTH), wo_bf,
      ln2_g.reshape(1, D_MODEL), wrh, wrl, br, ltri)


def _dispatch_kernel(fill_ref, has_ref, nu_ref, idx_ref, h2_ref, out_hbm, zbuf, sem, fsem):
    i = pl.program_id(0)
    n_asg = idx_ref.shape[2]
    rt = ROW_TILES

    def fill_copy(e):
        start = pl.multiple_of(fill_ref[e] * rt, rt)
        return pltpu.make_async_copy(zbuf, out_hbm.at[pl.ds(start, zbuf.shape[0])], fsem)

    def tail_copy(b):
        start = pl.multiple_of(b * zbuf.shape[0], zbuf.shape[0])
        return pltpu.make_async_copy(zbuf, out_hbm.at[pl.ds(start, zbuf.shape[0])], fsem)

    @pl.when(i == 0)
    def _():
        zbuf[...] = jnp.zeros(zbuf.shape, F32)
        for e in range(N_EXPERTS):
            @pl.when(has_ref[e] > 0)
            def _():
                fill_copy(e).start()
        for e in range(N_EXPERTS):
            @pl.when(has_ref[e] > 0)
            def _():
                fill_copy(e).wait()
        n_blocks = out_hbm.shape[0] // zbuf.shape[0]

        def tail(b, carry):
            tail_copy(b).start()
            tail_copy(b).wait()
            return carry
        lax.fori_loop(nu_ref[0], n_blocks, tail, 0)

    def body(r8, carry):
        for u in range(DMA_UNROLL):
            r = r8 * DMA_UNROLL + u
            src = pl.multiple_of((r >> 1) * rt, rt)
            dst = pl.multiple_of(idx_ref[0, 0, r] * rt, rt)
            pltpu.make_async_copy(h2_ref.at[pl.ds(src, rt)], out_hbm.at[pl.ds(dst, rt)],
                                  sem).start(priority=u % 2)
        return carry
    lax.fori_loop(0, n_asg // DMA_UNROLL, body, 0)
    for _ in range(2):
        pltpu.make_async_copy(h2_ref, out_hbm.at[pl.ds(0, h2_ref.shape[0])], sem).wait()


def _dispatch(h2, slot_of, n_slots, fill_at, has_fill, n_used):
    tm = TM_PROJ
    n_steps = h2.shape[0] // (tm * ROW_TILES)
    grid_spec = pltpu.PrefetchScalarGridSpec(
        num_scalar_prefetch=3,
        grid=(n_steps,),
        in_specs=[pl.BlockSpec((1, 1, 2 * tm), lambda i, fa, hf, nu: (i, 0, 0), memory_space=pltpu.SMEM),
                  pl.BlockSpec((tm * ROW_TILES, LANES), lambda i, fa, hf, nu: (i, 0))],
        out_specs=pl.BlockSpec(memory_space=pl.ANY),
        scratch_shapes=[pltpu.VMEM((BM_MOE * ROW_TILES, LANES), F32),
                        pltpu.SemaphoreType.DMA(()), pltpu.SemaphoreType.DMA(())],
    )
    return pl.pallas_call(
        _dispatch_kernel,
        grid_spec=grid_spec,
        out_shape=jax.ShapeDtypeStruct((n_slots * ROW_TILES, LANES), F32),
        compiler_params=_cparams(("arbitrary",)),
        name="moe_dispatch",
    )(fill_at, has_fill, n_used, slot_of.reshape(n_steps, 1, 2 * tm), h2)


def _moe_kernel(be_ref, nu_ref, x_ref, w1_ref, w3_ref, w2_ref, y_ref):
    del be_ref
    bm = BM_MOE
    rt = ROW_TILES

    @pl.when(pl.program_id(0) < nu_ref[0])
    def _():
        x = jnp.concatenate([x_ref[pl.ds(c, bm, stride=rt), :] for c in range(rt)], axis=1).astype(BF16)
        a = jnp.dot(x, w1_ref[...], preferred_element_type=F32)
        b = jnp.dot(x, w3_ref[...], preferred_element_type=F32)
        hmid = ((a / (1.0 + jnp.exp(-a))) * b).astype(BF16)
        y = jnp.dot(hmid, w2_ref[...], preferred_element_type=F32)
        for c in range(rt):
            y_ref[pl.ds(c, bm, stride=rt), :] = y[:, c * LANES:(c + 1) * LANES]

    @pl.when(pl.program_id(0) >= nu_ref[0])
    def _():
        y_ref[...] = jnp.zeros(y_ref.shape, F32)


def _moe(xs, block_e, n_used, w1_bf, w3_bf, w2_bf):
    bm = BM_MOE
    n_blocks = block_e.shape[0]
    row_in = pl.BlockSpec((bm * ROW_TILES, LANES), lambda i, be, nu: (jnp.minimum(i, nu[0] - 1), 0))
    row_out = pl.BlockSpec((bm * ROW_TILES, LANES), lambda i, be, nu: (i, 0))
    grid_spec = pltpu.PrefetchScalarGridSpec(
        num_scalar_prefetch=2,
        grid=(n_blocks,),
        in_specs=[
            row_in,
            pl.BlockSpec((None, D_MODEL, D_EXPERT), lambda i, be, nu: (be[i], 0, 0)),
            pl.BlockSpec((None, D_MODEL, D_EXPERT), lambda i, be, nu: (be[i], 0, 0)),
            pl.BlockSpec((None, D_EXPERT, D_MODEL), lambda i, be, nu: (be[i], 0, 0)),
        ],
        out_specs=row_out,
    )
    return pl.pallas_call(
        _moe_kernel,
        grid_spec=grid_spec,
        out_shape=jax.ShapeDtypeStruct(xs.shape, F32),
        compiler_params=_cparams(("arbitrary",)),
        name="moe_experts",
    )(block_e, n_used, xs, w1_bf, w3_bf, w2_bf)


def _moe_plan(rt, cnt, tm, bm):
    t_tok = rt.shape[0]
    expert = rt[:, 2:4].astype(I32)
    rank_in_tile = rt[:, 4:6].astype(I32)
    tile_cnt = cnt[:, 0, :N_EXPERTS].astype(I32)
    tile_off = jnp.cumsum(tile_cnt, axis=0) - tile_cnt
    counts = jnp.sum(tile_cnt, axis=0)
    pcounts = (counts + bm - 1) // bm * bm
    pends = jnp.cumsum(pcounts)
    pstarts = pends - pcounts
    base = jnp.repeat(pstarts[None, :] + tile_off, tm, axis=0)
    onehot = expert[:, :, None] == jnp.arange(N_EXPERTS, dtype=I32)[None, None, :]
    dest = jnp.sum(jnp.where(onehot, base[:, None, :], 0), axis=-1) + rank_in_tile
    n_blocks = 2 * t_tok // bm + N_EXPERTS
    blk_start = jnp.arange(n_blocks, dtype=I32) * bm
    block_e = jnp.minimum(jnp.sum((pends[None, :] <= blk_start[:, None]).astype(I32), axis=1), N_EXPERTS - 1)
    n_used = (pends[-1] // bm).astype(I32).reshape(1)
    fill_at = jnp.maximum(pends - bm, 0).astype(I32)
    has_fill = (pcounts > 0).astype(I32)
    return dest.reshape(-1).astype(I32), block_e.astype(I32), n_used, fill_at, has_fill


def _final_kernel(idxc_ref, idxn_ref, x1_ref, ys_hbm, rt_ref, mod_ref, modf_ref, g_ref, o_ref, ybuf, sem):
    i = pl.program_id(0)
    n = pl.num_programs(0)
    slot = i % 2
    rt = rt_ref[...]
    tm = rt.shape[0]
    n_asg = 2 * tm
    rtl = ROW_TILES

    def start_gather(idx_ref, dst_slot):
        def body(r8, carry):
            for u in range(DMA_UNROLL):
                r = r8 * DMA_UNROLL + u
                src = pl.multiple_of(idx_ref[0, 0, r] * rtl, rtl)
                pltpu.make_async_copy(ys_hbm.at[pl.ds(src, rtl)],
                                      ybuf.at[dst_slot, pl.ds(pl.multiple_of(r * rtl, rtl), rtl)],
                                      sem.at[dst_slot]).start(priority=u % 2)
            return carry
        lax.fori_loop(0, n_asg // DMA_UNROLL, body, 0)

    @pl.when(i == 0)
    def _():
        start_gather(idxc_ref, 0)

    @pl.when(i + 1 < n)
    def _():
        start_gather(idxn_ref, 1 - slot)

    pltpu.make_async_copy(ys_hbm.at[pl.ds(0, n_asg * rtl)], ybuf.at[slot], sem.at[slot]).wait()

    g1, g2 = rt[:, 0:1], rt[:, 1:2]
    moe = jnp.concatenate(
        [g1 * ybuf[slot, pl.ds(c, tm, stride=2 * rtl), :]
         + g2 * ybuf[slot, pl.ds(rtl + c, tm, stride=2 * rtl), :] for c in range(rtl)], axis=1)
    x2 = x1_ref[...] + mod_ref[0][5:6] * moe
    modf = modf_ref[0]
    inv = lax.rsqrt(jnp.mean(x2 * x2, axis=-1, keepdims=True) + EPS)
    o_ref[...] = (x2 * inv) * (g_ref[...] * (1.0 + modf[1:2])) + modf[0:1]


def _final(x1, ys, slot_of, rt, mod, modf, boff, seq, lnf_g):
    t_tok = x1.shape[0]
    tm = TM_FIN
    n_steps = t_tok // tm
    per_seq = seq // tm
    tok = lambda i: (i, 0)
    idx3 = slot_of.reshape(n_steps, 1, 2 * tm)
    smem_blk = lambda fn: pl.BlockSpec((1, 1, 2 * tm), fn, memory_space=pltpu.SMEM)
    return pl.pallas_call(
        _final_kernel,
        grid=(n_steps,),
        in_specs=[smem_blk(lambda i: (i, 0, 0)),
                  smem_blk(lambda i: (jnp.minimum(i + 1, n_steps - 1), 0, 0)),
                  pl.BlockSpec((tm, D_MODEL), tok),
                  pl.BlockSpec(memory_space=pl.ANY),
                  pl.BlockSpec((tm, LANES), tok),
                  pl.BlockSpec((1, 6, D_MODEL), lambda i: (boff + i // per_seq, 0, 0)),
                  pl.BlockSpec((1, 2, D_MODEL), lambda i: (boff + i // per_seq, 0, 0)),
                  pl.BlockSpec((1, D_MODEL), lambda i: (0, 0))],
        out_specs=pl.BlockSpec((tm, D_MODEL), tok),
        out_shape=jax.ShapeDtypeStruct((t_tok, D_MODEL), F32),
        scratch_shapes=[pltpu.VMEM((2, tm * 2 * ROW_TILES, LANES), F32), pltpu.SemaphoreType.DMA((2,))],
        compiler_params=_cparams(("arbitrary",)),
        name="combine_final",
    )(idx3, idx3, x1, ys, rt, mod, modf, lnf_g.reshape(1, D_MODEL))


def _trunk(x, boff, mod, modf, prm):
    batch, seq, _ = x.shape
    x2d = x.reshape(batch * seq, D_MODEL)
    ta, tb = _rope_tables(seq)
    qa, ka, va, qb8, kb, vb = _inproj(x2d, mod, boff, seq, prm["ln1_g"], prm["w_in"], ta, tb,
                                      prm["qg"], prm["kg"], prm["bd"])
    oa = _attn_a(qa, ka, va, batch, seq)
    ob = _attn_b(qb8, kb, vb, batch, seq)
    x1, h2, rt, cnt = _outproj(x2d, oa, ob, mod, boff, seq, prm["on_a_g"], prm["on_b_g"], prm["w_out"],
                               prm["ln2_g"], prm["wrh"], prm["wrl"], prm["br"], prm["ltri"])
    dest, block_e, n_used, fill_at, has_fill = _moe_plan(rt, cnt, TM_PROJ, BM_MOE)
    xs = _dispatch(h2, dest, block_e.shape[0] * BM_MOE, fill_at, has_fill, n_used)
    ys = _moe(xs, block_e, n_used, prm["w1"], prm["w3"], prm["w2"])
    y = _final(x1, ys, dest, rt, mod, modf, boff, seq, prm["lnf_g"])
    return y.reshape(batch, seq, D_MODEL)


def kernel(x_prompt, x_sample, c_prompt, c_sample, ln1_g, ln2_g, w_ada, b_ada, w_in, w_out, qn_g, kn_g,
           on_a_g, on_b_g, w_rg, b_rg, w_re, b_re, w1, w3, w2, lnf_g, w_adaf, b_adaf):
    assert ln1_g.shape[0] == 1, "single-layer trunk"
    nb_p, nb_s = c_prompt.shape[0], c_sample.shape[0]
    rows = -(-(nb_p + nb_s) // 8) * 8
    c_all = jnp.concatenate([c_prompt, c_sample, jnp.zeros((rows - nb_p - nb_s, D_MODEL), F32)], axis=0)
    mod = _modulation(c_all, w_ada[0], b_ada[0]).reshape(rows, 6, D_MODEL)
    modf = _modulation(c_all, w_adaf, b_adaf).reshape(rows, 2, D_MODEL)

    wr = jnp.concatenate([w_rg[0], w_re[0], jnp.zeros((D_MODEL, LANES - N_GROUPS - N_EXPERTS), F32)], axis=1)
    wrh = wr.astype(BF16)
    wrl = (wr - wrh.astype(F32)).astype(BF16)
    br = jnp.concatenate([b_rg[0], b_re[0], jnp.zeros((LANES - N_GROUPS - N_EXPERTS,), F32)]).reshape(1, LANES)
    seg = lax.broadcasted_iota(I32, (2 * LANES, LANES), 0) % LANES // HEAD_DIM
    bd = (seg == lax.broadcasted_iota(I32, (2 * LANES, LANES), 1) // HEAD_DIM).astype(BF16)
    ltri = (lax.broadcasted_iota(I32, (TM_PROJ, TM_PROJ), 1)
            < lax.broadcasted_iota(I32, (TM_PROJ, TM_PROJ), 0)).astype(BF16)
    prm = dict(
        ltri=ltri,
        ln1_g=ln1_g[0], ln2_g=ln2_g[0], w_in=w_in[0].astype(BF16), w_out=w_out[0].astype(BF16),
        qg=jnp.tile(qn_g[0], 2).reshape(1, LANES), kg=jnp.tile(kn_g[0], 2).reshape(1, LANES), bd=bd,
        on_a_g=on_a_g[0], on_b_g=on_b_g[0], wrh=wrh, wrl=wrl, br=br,
        w1=w1[0].astype(BF16), w3=w3[0].astype(BF16), w2=w2[0].astype(BF16), lnf_g=lnf_g,
    )
    y_prompt = _trunk(x_prompt, 0, mod, modf, prm)
    y_sample = _trunk(x_sample, nb_p, mod, modf, prm)
    return (y_prompt, y_sample)
```

```python
import functools

import jax
import jax.numpy as jnp
from jax import lax
from jax.experimental import pallas as pl
from jax.experimental.pallas import tpu as pltpu

F32 = jnp.float32
BF16 = jnp.bfloat16
I32 = jnp.int32

D_MODEL = 1024
HEAD_DIM = 64
N_HEADS = 8
WIDTH = N_HEADS * HEAD_DIM
KV_WIDTH = 2 * HEAD_DIM
IN_COLS = 3 * WIDTH + WIDTH + 2 * KV_WIDTH
ROPE_THETA = 500000.0
ROPE_DIMS_A = 16
AXIAL_THETA = 10000.0
GRID_W = 64
N_GROUPS = 4
EXPERTS_PER_GROUP = 8
N_EXPERTS = 32
D_EXPERT = 512
EPS = 1e-6
NEG = -1e30
DIL_W = 64
A_BATCH = 8

LANES = 128
VMEM_LIMIT = 56 * 1024 * 1024

TM_PROJ = 512
TILE_A = 1024
TQ_B = 256
NQ_B = 4
B_ITEMS_PER_TRIP = 8
BF16_ROWS = 16
LOG2E = 1.4426950408889634
ROW_TILES = D_MODEL // LANES
BM_MOE = 256
DMA_UNROLL = 8
TM_FIN = 512


def _cparams(sem):
    return pltpu.CompilerParams(dimension_semantics=sem, vmem_limit_bytes=VMEM_LIMIT)


def _mod_kernel(c_ref, w_ref, b_ref, o_ref):
    c = c_ref[...]
    a = c / (1.0 + jnp.exp(-c))
    o_ref[...] = jnp.dot(a, w_ref[...], preferred_element_type=F32,
                         precision=lax.Precision.HIGHEST) + b_ref[...]


def _modulation(c_pad, w, b):
    rows, d = c_pad.shape
    n = w.shape[1]
    tn = 1024
    return pl.pallas_call(
        _mod_kernel,
        grid=(n // tn,),
        in_specs=[pl.BlockSpec((rows, d), lambda j: (0, 0)),
                  pl.BlockSpec((d, tn), lambda j: (0, j)),
                  pl.BlockSpec((1, tn), lambda j: (0, j))],
        out_specs=pl.BlockSpec((rows, tn), lambda j: (0, j)),
        out_shape=jax.ShapeDtypeStruct((rows, n), F32),
        compiler_params=_cparams(("arbitrary",)),
        name="modulation",
    )(c_pad, w, b.reshape(1, n))


def _rope_tables(seq):
    t = jnp.arange(seq)
    tf = t.astype(F32)
    inv_a = ROPE_THETA ** (-jnp.arange(0, ROPE_DIMS_A, 2, dtype=F32) / ROPE_DIMS_A)
    ang = tf[:, None] * inv_a[None, :]
    cos, sin = jnp.cos(ang), jnp.sin(ang)
    rest = HEAD_DIM - ROPE_DIMS_A
    one = jnp.ones((seq, rest), F32)
    zero = jnp.zeros((seq, rest), F32)
    z8 = jnp.zeros_like(sin)
    cos64 = jnp.concatenate([cos, cos, one], 1)
    sm64 = jnp.concatenate([-sin, z8, zero], 1)
    sp64 = jnp.concatenate([z8, sin, zero], 1)
    ta = jnp.stack([jnp.tile(a, (1, 2)) for a in (cos64, sm64, sp64)])

    n_ax = HEAD_DIM // 2
    inv_b = AXIAL_THETA ** (-jnp.arange(0, n_ax, 2, dtype=F32) / n_ax)
    row_pos = (t // GRID_W).astype(F32)
    col_pos = (t % GRID_W).astype(F32)
    ar = row_pos[:, None] * inv_b[None, :]
    ac = col_pos[:, None] * inv_b[None, :]
    z16 = jnp.zeros_like(ar)
    cos64 = jnp.concatenate([jnp.cos(ar), jnp.cos(ar), jnp.cos(ac), jnp.cos(ac)], 1)
    sm64 = jnp.concatenate([-jnp.sin(ar), z16, -jnp.sin(ac), z16], 1)
    sp64 = jnp.concatenate([z16, jnp.sin(ar), z16, jnp.sin(ac)], 1)
    tb = jnp.stack([jnp.tile(a, (1, 2)) for a in (cos64, sm64, sp64)])
    return ta, tb


def _inproj_kernel(x_ref, mod_ref, g_ref, w_ref, ta_ref, tb_ref, qg_ref, kg_ref, bd_ref,
                   qa_ref, ka_ref, va_ref, qb_ref, kb_ref, vb_ref):
    x = x_ref[...]
    mod = mod_ref[0]
    inv = lax.rsqrt(jnp.mean(x * x, axis=-1, keepdims=True) + EPS)
    h = (x * inv) * (g_ref[...] * (1.0 + mod[1:2])) + mod[0:1]
    hb = h.astype(BF16)
    cos_a, sm_a, sp_a = ta_ref[0], ta_ref[1], ta_ref[2]
    cos_b, sm_b, sp_b = tb_ref[0], tb_ref[1], tb_ref[2]
    bd = bd_ref[...]
    lane = lax.broadcasted_iota(I32, (1, LANES), 1)
    low = lane < HEAD_DIM

    def proj(lo, n):
        return jnp.dot(hb, w_ref[:, lo:lo + n], preferred_element_type=F32)

    def rope(y, cos, sm, sp, sh):
        return y * cos + pltpu.roll(y, LANES - sh, 1) * sm + pltpu.roll(y, sh, 1) * sp

    def headnorm(y, g):
        t = y * y
        thi = t.astype(BF16)
        tlo = (t - thi.astype(F32)).astype(BF16)
        ss = jnp.dot(jnp.concatenate([thi, tlo], axis=1), bd, preferred_element_type=F32)
        return y * lax.rsqrt(ss * (1.0 / HEAD_DIM) + EPS) * g

    scale = HEAD_DIM ** -0.5
    qa = proj(0, WIDTH)
    ka = proj(WIDTH, WIDTH)
    for c in range(WIDTH // LANES):
        sl = slice(c * LANES, (c + 1) * LANES)
        qa_ref[:, sl] = rope(qa[:, sl], cos_a, sm_a, sp_a, 8) * (scale * LOG2E)
        ka_ref[:, sl] = rope(ka[:, sl], cos_a, sm_a, sp_a, 8)
    va_ref[...] = proj(2 * WIDTH, WIDTH)

    qb = proj(3 * WIDTH, WIDTH)
    qg = qg_ref[...]
    for c in range(WIDTH // LANES):
        y = rope(headnorm(qb[:, c * LANES:(c + 1) * LANES], qg), cos_b, sm_b, sp_b, 16) * (scale * LOG2E)
        ysw = pltpu.roll(y, HEAD_DIM, 1)
        grp = c // 2
        h0 = y if grp == 0 else ysw
        h1 = ysw if grp == 0 else y
        keep = low if grp == 0 else jnp.logical_not(low)
        for u, hh in enumerate((h0, h1)):
            ht = jnp.where(keep, hh, 0.0).T.astype(BF16)
            for sub in range(ht.shape[1] // TQ_B):
                qb_ref[sub, 2 * c + u] = ht[:, sub * TQ_B:(sub + 1) * TQ_B]
    kb = proj(4 * WIDTH, KV_WIDTH)
    kb_ref[...] = rope(headnorm(kb, kg_ref[...]), cos_b, sm_b, sp_b, 16).astype(BF16)
    vb_ref[0] = proj(4 * WIDTH + KV_WIDTH, KV_WIDTH).T.astype(BF16)


def _inproj(x2d, mod, boff, seq, ln1_g, w_in_bf, ta, tb, qg, kg, bd):
    t_tok = x2d.shape[0]
    tm = TM_PROJ
    per_seq = seq // tm
    tok = lambda i: (i, 0)
    const = lambda i: (0, 0)
    return pl.pallas_call(
        _inproj_kernel,
        grid=(t_tok // tm,),
        in_specs=[
            pl.BlockSpec((tm, D_MODEL), tok),
            pl.BlockSpec((1, 6, D_MODEL), lambda i: (boff + i // per_seq, 0, 0)),
            pl.BlockSpec((1, D_MODEL), const),
            pl.BlockSpec((D_MODEL, IN_COLS), const),
            pl.BlockSpec((3, tm, LANES), lambda i: (0, i % per_seq, 0)),
            pl.BlockSpec((3, tm, LANES), lambda i: (0, i % per_seq, 0)),
            pl.BlockSpec((1, LANES), const),
            pl.BlockSpec((1, LANES), const),
            pl.BlockSpec((2 * LANES, LANES), const),
        ],
        out_specs=[
            pl.BlockSpec((tm, WIDTH), tok),
            pl.BlockSpec((tm, WIDTH), tok),
            pl.BlockSpec((tm, WIDTH), tok),
            pl.BlockSpec((tm // TQ_B, N_HEADS, LANES, TQ_B), lambda i: (i, 0, 0, 0)),
            pl.BlockSpec((tm, KV_WIDTH), tok),
            pl.BlockSpec((1, KV_WIDTH, tm), lambda i: (i, 0, 0)),
        ],
        out_shape=[
            jax.ShapeDtypeStruct((t_tok, WIDTH), F32),
            jax.ShapeDtypeStruct((t_tok, WIDTH), F32),
            jax.ShapeDtypeStruct((t_tok, WIDTH), F32),
            jax.ShapeDtypeStruct((t_tok // TQ_B, N_HEADS, LANES, TQ_B), BF16),
            jax.ShapeDtypeStruct((t_tok, KV_WIDTH), BF16),
            jax.ShapeDtypeStruct((t_tok // tm, KV_WIDTH, tm), BF16),
        ],
        compiler_params=_cparams(("parallel",)),
        name="inproj",
    )(x2d, mod, ln1_g.reshape(1, D_MODEL), w_in_bf, ta, tb, qg, kg, bd)


def _attn_a_kernel(q_ref, kp_ref, kc_ref, kn_ref, vp_ref, vc_ref, vn_ref, o_ref,
                   m1, l1, a1, m2, l2, a2, m3, l3, a3, *, seq):
    n = pl.program_id(1)
    t = TILE_A
    w = DIL_W

    lane = lax.broadcasted_iota(I32, (1, LANES), 1)
    low = lane < HEAD_DIM
    row_i = lax.broadcasted_iota(I32, (2 * w, 1), 0) & (w - 1)
    col_m = lax.broadcasted_iota(I32, (1, 3 * w), 1)
    band_bias = jnp.where(jnp.abs(col_m - w - row_i) <= w, 0.0, NEG)
    base = n * t - t

    def window_rows(refs, start, count, d):
        parts = []
        while count > 0:
            ref = refs[start // t]
            local = start % t
            take = min(count, -(-(t - local) // d))
            parts.append(ref[pl.ds(local, take, stride=d), :] if d > 1 else ref[pl.ds(local, take), :])
            start += take * d
            count -= take
        return parts[0] if len(parts) == 1 else jnp.concatenate(parts, axis=0)

    units = []
    for d, m_s, l_s, a_s in ((1, m1, l1, a1), (4, m2, l2, a2), (16, m3, l3, a3)):
        for j in range(t // w):
            qs = (j // d) * (w * d) + (j % d)
            units.append((d, qs, m_s, l_s, a_s))

    def rows(qs, d):
        return pl.ds(qs, w, stride=d) if d > 1 else pl.ds(qs, w)

    def score(unit):
        d, qs = unit[0], unit[1]
        ks = qs + t - w * d
        q = q_ref[rows(qs, d), :]
        k = window_rows((kp_ref, kc_ref, kn_ref), ks, 3 * w, d).astype(BF16)
        q2 = jnp.concatenate([jnp.where(low, q, 0.0), jnp.where(low, 0.0, q)], axis=0).astype(BF16)
        s = lax.dot_general(q2, k, (((1,), (1,)), ((), ())), preferred_element_type=F32) + band_bias
        if ks >= t and ks + (3 * w - 1) * d < 2 * t:
            return s
        kpos = base + ks + d * col_m
        edge_bias = jnp.where(kpos >= 0, jnp.where(kpos < seq, 0.0, NEG), NEG)
        return s + edge_bias

    def finish(unit, s):
        d, qs, m_s, l_s, a_s = unit
        ks = qs + t - w * d
        v = window_rows((vp_ref, vc_ref, vn_ref), ks, 3 * w, d).astype(BF16)
        mx = jnp.max(s, axis=-1, keepdims=True)
        p = jnp.exp2(s - mx)
        den = jnp.sum(p, axis=-1, keepdims=True)
        o = jnp.dot(p.astype(BF16), v, preferred_element_type=F32)
        m_s[rows(qs, d), :] = jnp.where(low, mx[0:w], mx[w:2 * w])
        l_s[rows(qs, d), :] = jnp.where(low, den[0:w], den[w:2 * w])
        a_s[rows(qs, d), :] = jnp.where(low, o[0:w], o[w:2 * w])

    nb = A_BATCH
    pending = [score(u) for u in units[:nb]]
    for b0 in range(0, len(units), nb):
        ahead = [score(u) for u in units[b0 + nb:b0 + 2 * nb]]
        for u, s in zip(units[b0:b0 + nb], pending):
            finish(u, s)
        pending = ahead

    mm = jnp.maximum(jnp.maximum(m1[...], m2[...]), m3[...])
    w1 = jnp.exp2(m1[...] - mm)
    w2 = jnp.exp2(m2[...] - mm)
    w3 = jnp.exp2(m3[...] - mm)
    num = w1 * a1[...] + w2 * a2[...] + w3 * a3[...]
    den = w1 * l1[...] + w2 * l2[...] + w3 * l3[...]
    o_ref[...] = (num / den).astype(o_ref.dtype)


def _attn_a(qa, ka, va, batch, seq):
    t = TILE_A
    nt = seq // t
    q3 = qa.reshape(batch, seq, WIDTH)
    k3 = ka.reshape(batch, seq, WIDTH)
    v3 = va.reshape(batch, seq, WIDTH)
    cur = lambda b, n, p: (b, n, p)
    prev = lambda b, n, p: (b, jnp.maximum(n - 1, 0), p)
    nxt = lambda b, n, p: (b, jnp.minimum(n + 1, nt - 1), p)
    blk = (None, t, LANES)
    scr = pltpu.VMEM((t, LANES), F32)
    out = pl.pallas_call(
        functools.partial(_attn_a_kernel, seq=seq),
        grid=(batch, nt, WIDTH // LANES),
        in_specs=[pl.BlockSpec(blk, cur),
                  pl.BlockSpec(blk, prev), pl.BlockSpec(blk, cur), pl.BlockSpec(blk, nxt),
                  pl.BlockSpec(blk, prev), pl.BlockSpec(blk, cur), pl.BlockSpec(blk, nxt)],
        out_specs=pl.BlockSpec(blk, cur),
        out_shape=jax.ShapeDtypeStruct((batch, seq, WIDTH), BF16),
        scratch_shapes=[scr] * 9,
        compiler_params=_cparams(("parallel", "parallel", "parallel")),
        name="attn_dilated",
    )(q3, k3, k3, k3, v3, v3, v3)
    return out.reshape(batch * seq, WIDTH)


def _attn_b_kernel(qt_ref, k_ref, vt_ref, o_ref, m_s, acc_s, s_a, s_b, *, nkv):
    nqt, tq = qt_ref.shape[0], qt_ref.shape[3]
    tk = k_ref.shape[1]
    m_s[...] = jnp.full(m_s.shape, NEG, F32)
    acc_s[...] = jnp.zeros(acc_s.shape, F32)
    fold = 8
    ones_rows = jnp.ones((BF16_ROWS, tk), BF16)
    log_nkv = nkv.bit_length() - 1

    def score(t, dst, h):
        dst[h] = jnp.dot(k_ref[t & (nkv - 1)], qt_ref[t >> log_nkv, h], preferred_element_type=F32)

    def update(t, src, h):
        qi, j = t >> log_nkv, t & (nkv - 1)
        cs = slice(tq * h, tq * (h + 1))
        r0 = HEAD_DIM * (h // (N_HEADS // 2))
        vt = jnp.concatenate([vt_ref[j, r0:r0 + HEAD_DIM, :], ones_rows], axis=0)
        s = src[h]
        part = jnp.max(s.reshape(fold, tk // fold, tq), axis=0)
        m_old = m_s[qi, :, cs]
        m_new = jnp.maximum(m_old, jnp.max(part, axis=0, keepdims=True))
        alpha = jnp.exp2(m_old - m_new)
        p = jnp.exp2((s - m_new).astype(BF16))
        acc_s[qi, :, cs] = alpha * acc_s[qi, :, cs] + jnp.dot(vt, p, preferred_element_type=F32)
        m_s[qi, :, cs] = m_new

    def step(t_next, dst, t_cur, src):
        for h in range(N_HEADS):
            if t_next is not None:
                score(t_next, dst, h)
            update(t_cur, src, h)

    n_items = nqt * nkv
    for h in range(N_HEADS):
        score(0, s_a, h)

    bufs = (s_a, s_b)
    per_trip = B_ITEMS_PER_TRIP

    def body(trip, carry):
        t0 = per_trip * trip
        for u in range(per_trip):
            step(t0 + u + 1, bufs[(u + 1) % 2], t0 + u, bufs[u % 2])
        return carry

    lax.fori_loop(0, n_items // per_trip - 1, body, 0)
    t0 = n_items - per_trip
    for u in range(per_trip - 1):
        step(t0 + u + 1, bufs[(u + 1) % 2], t0 + u, bufs[u % 2])
    step(None, None, n_items - 1, bufs[(per_trip - 1) % 2])

    for qi in range(nqt):
        o = acc_s[qi, 0:HEAD_DIM] / acc_s[qi, HEAD_DIM:HEAD_DIM + 1]
        for c in range(N_HEADS // 2):
            pair = jnp.concatenate([o[:, (2 * c) * tq:(2 * c + 1) * tq],
                                    o[:, (2 * c + 1) * tq:(2 * c + 2) * tq]], axis=0)
            o_ref[qi * tq:(qi + 1) * tq, c * LANES:(c + 1) * LANES] = pair.T.astype(o_ref.dtype)


def _attn_b(qbt, kb, vbt, batch, seq):
    tq, tk, nqt = TQ_B, TM_PROJ, NQ_B
    steps = seq // (tq * nqt)
    nkv = seq // tk
    assert nkv & (nkv - 1) == 0 and nkv >= 2
    sbuf = pltpu.VMEM((N_HEADS, tk, tq), F32)
    k3 = kb.reshape(batch * nkv, tk, KV_WIDTH)
    return pl.pallas_call(
        functools.partial(_attn_b_kernel, nkv=nkv),
        grid=(batch, steps),
        in_specs=[pl.BlockSpec((nqt, N_HEADS, LANES, tq), lambda b, i: (b * steps + i, 0, 0, 0)),
                  pl.BlockSpec((nkv, tk, KV_WIDTH), lambda b, i: (b, 0, 0)),
                  pl.BlockSpec((nkv, KV_WIDTH, tk), lambda b, i: (b, 0, 0))],
        out_specs=pl.BlockSpec((nqt * tq, WIDTH), lambda b, i: (b * steps + i, 0)),
        out_shape=jax.ShapeDtypeStruct((batch * seq, WIDTH), BF16),
        scratch_shapes=[pltpu.VMEM((nqt, 1, N_HEADS * tq), F32),
                        pltpu.VMEM((nqt, HEAD_DIM + BF16_ROWS, N_HEADS * tq), F32), sbuf, sbuf],
        compiler_params=_cparams(("parallel", "parallel")),
        name="attn_dense",
    )(qbt, k3, vbt)


def _outproj_kernel(x_ref, oa_ref, ob_ref, mod_ref, ga_ref, gb_ref, wo_ref, ln2_ref,
                    wr_ref, br_ref, ltri_ref, x1_ref, h2_ref, rt_ref, cnt_ref):
    def rms(o, g):
        return o * lax.rsqrt(jnp.mean(o * o, axis=-1, keepdims=True) + EPS) * g

    oa = rms(oa_ref[...].astype(F32), ga_ref[...])
    ob = rms(ob_ref[...].astype(F32), gb_ref[...])
    o = jnp.concatenate([oa, ob], axis=1).astype(BF16)
    mod = mod_ref[0]
    x1 = x_ref[...] + mod[2:3] * jnp.dot(o, wo_ref[...], preferred_element_type=F32)
    x1_ref[...] = x1
    inv = lax.rsqrt(jnp.mean(x1 * x1, axis=-1, keepdims=True) + EPS)
    h2 = (x1 * inv) * (ln2_ref[...] * (1.0 + mod[4:5])) + mod[3:4]
    tm = h2.shape[0]
    for c in range(ROW_TILES):
        h2_ref[pl.ds(c, tm, stride=ROW_TILES), :] = h2[:, c * LANES:(c + 1) * LANES]

    hh = h2.astype(BF16)
    hl = (h2 - hh.astype(F32)).astype(BF16)
    both = jnp.dot(hh, wr_ref[...], preferred_element_type=F32)
    logits = (both[:, 0:LANES] + both[:, LANES:2 * LANES]
              + jnp.dot(hl, wr_ref[:, 0:LANES], preferred_element_type=F32)) + br_ref[...]

    lane = lax.broadcasted_iota(I32, logits.shape, 1).astype(F32)
    big = jnp.float32(LANES)
    is_g = lane < N_GROUPS
    gl = jnp.where(is_g, logits, -jnp.inf)
    gmax = jnp.max(gl, axis=-1, keepdims=True)
    g_sel = jnp.min(jnp.where(gl == gmax, lane, big), axis=-1, keepdims=True)
    g_den = jnp.sum(jnp.exp(gl - gmax), axis=-1, keepdims=True)
    g_w = 1.0 / g_den
    e_lo = N_GROUPS + EXPERTS_PER_GROUP * g_sel
    is_e = (lane >= e_lo) & (lane < e_lo + EXPERTS_PER_GROUP)
    el = jnp.where(is_e, logits, -jnp.inf)
    emax = jnp.max(el, axis=-1, keepdims=True)
    p = jnp.where(is_e, jnp.exp(el - emax), -1.0)
    p1 = jnp.max(p, axis=-1, keepdims=True)
    i1 = jnp.min(jnp.where(p == p1, lane, big), axis=-1, keepdims=True)
    pm = jnp.where(lane == i1, -1.0, p)
    p2 = jnp.max(pm, axis=-1, keepdims=True)
    i2 = jnp.min(jnp.where(pm == p2, lane, big), axis=-1, keepdims=True)
    tot = p1 + p2
    gate1 = g_w * p1 / tot
    gate2 = g_w * p2 / tot
    e1 = i1 - N_GROUPS
    e2 = i2 - N_GROUPS
    memb = jnp.where(lane == e1, 1.0, jnp.where(lane == e2, 1.0, 0.0))
    before = jnp.dot(ltri_ref[...], memb.astype(BF16), preferred_element_type=F32)
    r1 = jnp.sum(jnp.where(lane == e1, before, 0.0), axis=-1, keepdims=True)
    r2 = jnp.sum(jnp.where(lane == e2, before, 0.0), axis=-1, keepdims=True)
    cnt_ref[0] = jnp.broadcast_to(jnp.sum(memb, axis=0, keepdims=True), cnt_ref.shape[1:])
    cols = (gate1, gate2, e1, e2, r1, r2)
    out = jnp.zeros_like(logits)
    for idx, col in enumerate(cols):
        out = jnp.where(lane == idx, col, out)
    rt_ref[...] = out


def _outproj(x2d, oa, ob, mod, boff, seq, ga, gb, wo_bf, ln2_g, wr_split, br, ltri):
    t_tok = x2d.shape[0]
    tm = TM_PROJ
    per_seq = seq // tm
    tok = lambda i: (i, 0)
    const = lambda i: (0, 0)
    return pl.pallas_call(
        _outproj_kernel,
        grid=(t_tok // tm,),
        in_specs=[
            pl.BlockSpec((tm, D_MODEL), tok),
            pl.BlockSpec((tm, WIDTH), tok),
            pl.BlockSpec((tm, WIDTH), tok),
            pl.BlockSpec((1, 6, D_MODEL), lambda i: (boff + i // per_seq, 0, 0)),
            pl.BlockSpec((1, WIDTH), const),
            pl.BlockSpec((1, WIDTH), const),
            pl.BlockSpec((D_MODEL, D_MODEL), const),
            pl.BlockSpec((1, D_MODEL), const),
            pl.BlockSpec((D_MODEL, 2 * LANES), const),
            pl.BlockSpec((1, LANES), const),
            pl.BlockSpec((tm, tm), const),
        ],
        out_specs=[pl.BlockSpec((tm, D_MODEL), tok),
                   pl.BlockSpec((tm * ROW_TILES, LANES), tok),
                   pl.BlockSpec((tm, LANES), tok),
                   pl.BlockSpec((1, 8, LANES), lambda i: (i, 0, 0))],
        out_shape=[jax.ShapeDtypeStruct((t_tok, D_MODEL), F32),
                   jax.ShapeDtypeStruct((t_tok * ROW_TILES, LANES), F32),
                   jax.ShapeDtypeStruct((t_tok, LANES), F32),
                   jax.ShapeDtypeStruct((t_tok // tm, 8, LANES), F32)],
        compiler_params=_cparams(("parallel",)),
        name="outproj_router",
    )(x2d, oa, ob, mod, ga.reshape(1, WIDTH), gb.reshape(1, WIDTH), wo_bf,
      ln2_g.reshape(1, D_MODEL), wr_split, br, ltri)


def _dispatch_kernel(fill_ref, has_ref, nu_ref, idx_ref, h2_ref, out_hbm, zbuf, sem, fsem):
    i = pl.program_id(0)
    n_asg = idx_ref.shape[2]
    rt = ROW_TILES

    def fill_copy(e):
        start = pl.multiple_of(fill_ref[e] * rt, rt)
        return pltpu.make_async_copy(zbuf, out_hbm.at[pl.ds(start, zbuf.shape[0])], fsem)

    def tail_copy(b):
        start = pl.multiple_of(b * zbuf.shape[0], zbuf.shape[0])
        return pltpu.make_async_copy(zbuf, out_hbm.at[pl.ds(start, zbuf.shape[0])], fsem)

    @pl.when(i == 0)
    def _():
        zbuf[...] = jnp.zeros(zbuf.shape, F32)
        for e in range(N_EXPERTS):
            @pl.when(has_ref[e] > 0)
            def _():
                fill_copy(e).start()
        for e in range(N_EXPERTS):
            @pl.when(has_ref[e] > 0)
            def _():
                fill_copy(e).wait()
        n_blocks = out_hbm.shape[0] // zbuf.shape[0]

        def tail(b, carry):
            tail_copy(b).start()
            tail_copy(b).wait()
            return carry
        lax.fori_loop(nu_ref[0], n_blocks, tail, 0)

    def body(r8, carry):
        for u in range(DMA_UNROLL):
            r = r8 * DMA_UNROLL + u
            src = pl.multiple_of((r >> 1) * rt, rt)
            dst = pl.multiple_of(idx_ref[0, 0, r] * rt, rt)
            pltpu.make_async_copy(h2_ref.at[pl.ds(src, rt)], out_hbm.at[pl.ds(dst, rt)],
                                  sem).start(priority=u % 2)
        return carry
    lax.fori_loop(0, n_asg // DMA_UNROLL, body, 0)
    for _ in range(2):
        pltpu.make_async_copy(h2_ref, out_hbm.at[pl.ds(0, h2_ref.shape[0])], sem).wait()


def _dispatch(h2, slot_of, n_slots, fill_at, has_fill, n_used):
    tm = TM_PROJ
    n_steps = h2.shape[0] // (tm * ROW_TILES)
    grid_spec = pltpu.PrefetchScalarGridSpec(
        num_scalar_prefetch=3,
        grid=(n_steps,),
        in_specs=[pl.BlockSpec((1, 1, 2 * tm), lambda i, fa, hf, nu: (i, 0, 0), memory_space=pltpu.SMEM),
                  pl.BlockSpec((tm * ROW_TILES, LANES), lambda i, fa, hf, nu: (i, 0))],
        out_specs=pl.BlockSpec(memory_space=pl.ANY),
        scratch_shapes=[pltpu.VMEM((BM_MOE * ROW_TILES, LANES), F32),
                        pltpu.SemaphoreType.DMA(()), pltpu.SemaphoreType.DMA(())],
    )
    return pl.pallas_call(
        _dispatch_kernel,
        grid_spec=grid_spec,
        out_shape=jax.ShapeDtypeStruct((n_slots * ROW_TILES, LANES), F32),
        compiler_params=_cparams(("arbitrary",)),
        name="moe_dispatch",
    )(fill_at, has_fill, n_used, slot_of.reshape(n_steps, 1, 2 * tm), h2)


def _moe_kernel(be_ref, nu_ref, x_ref, w1_ref, w3_ref, w2_ref, y_ref):
    del be_ref
    bm = BM_MOE
    rt = ROW_TILES

    @pl.when(pl.program_id(0) < nu_ref[0])
    def _():
        x = jnp.concatenate([x_ref[pl.ds(c, bm, stride=rt), :] for c in range(rt)], axis=1).astype(BF16)
        a = jnp.dot(x, w1_ref[...], preferred_element_type=F32)
        b = jnp.dot(x, w3_ref[...], preferred_element_type=F32)
        hmid = ((a / (1.0 + jnp.exp(-a))) * b).astype(BF16)
        y = jnp.dot(hmid, w2_ref[...], preferred_element_type=F32)
        for c in range(rt):
            y_ref[pl.ds(c, bm, stride=rt), :] = y[:, c * LANES:(c + 1) * LANES]

    @pl.when(pl.program_id(0) >= nu_ref[0])
    def _():
        y_ref[...] = jnp.zeros(y_ref.shape, F32)


def _moe(xs, block_e, n_used, w1_bf, w3_bf, w2_bf):
    bm = BM_MOE
    n_blocks = block_e.shape[0]
    row_in = pl.BlockSpec((bm * ROW_TILES, LANES), lambda i, be, nu: (jnp.minimum(i, nu[0] - 1), 0))
    row_out = pl.BlockSpec((bm * ROW_TILES, LANES), lambda i, be, nu: (i, 0))
    grid_spec = pltpu.PrefetchScalarGridSpec(
        num_scalar_prefetch=2,
        grid=(n_blocks,),
        in_specs=[
            row_in,
            pl.BlockSpec((None, D_MODEL, D_EXPERT), lambda i, be, nu: (be[i], 0, 0)),
            pl.BlockSpec((None, D_MODEL, D_EXPERT), lambda i, be, nu: (be[i], 0, 0)),
            pl.BlockSpec((None, D_EXPERT, D_MODEL), lambda i, be, nu: (be[i], 0, 0)),
        ],
        out_specs=row_out,
    )
    return pl.pallas_call(
        _moe_kernel,
        grid_spec=grid_spec,
        out_shape=jax.ShapeDtypeStruct(xs.shape, F32),
        compiler_params=_cparams(("arbitrary",)),
        name="moe_experts",
    )(block_e, n_used, xs, w1_bf, w3_bf, w2_bf)


def _moe_plan(rt, cnt, tm, bm):
    t_tok = rt.shape[0]
    expert = rt[:, 2:4].astype(I32)
    rank_in_tile = rt[:, 4:6].astype(I32)
    tile_cnt = cnt[:, 0, :N_EXPERTS].astype(I32)
    tile_off = jnp.cumsum(tile_cnt, axis=0) - tile_cnt
    counts = jnp.sum(tile_cnt, axis=0)
    pcounts = (counts + bm - 1) // bm * bm
    pends = jnp.cumsum(pcounts)
    pstarts = pends - pcounts
    base = jnp.repeat(pstarts[None, :] + tile_off, tm, axis=0)
    onehot = expert[:, :, None] == jnp.arange(N_EXPERTS, dtype=I32)[None, None, :]
    dest = jnp.sum(jnp.where(onehot, base[:, None, :], 0), axis=-1) + rank_in_tile
    n_blocks = 2 * t_tok // bm + N_EXPERTS
    blk_start = jnp.arange(n_blocks, dtype=I32) * bm
    block_e = jnp.minimum(jnp.sum((pends[None, :] <= blk_start[:, None]).astype(I32), axis=1), N_EXPERTS - 1)
    n_used = (pends[-1] // bm).astype(I32).reshape(1)
    fill_at = jnp.maximum(pends - bm, 0).astype(I32)
    has_fill = (pcounts > 0).astype(I32)
    return dest.reshape(-1).astype(I32), block_e.astype(I32), n_used, fill_at, has_fill


def _final_kernel(idxc_ref, idxn_ref, x1_ref, ys_hbm, rt_ref, mod_ref, modf_ref, g_ref, o_ref, ybuf, sem):
    i = pl.program_id(0)
    n = pl.num_programs(0)
    slot = i % 2
    rt = rt_ref[...]
    tm = rt.shape[0]
    n_asg = 2 * tm
    rtl = ROW_TILES

    def start_gather(idx_ref, dst_slot):
        def body(r8, carry):
            for u in range(DMA_UNROLL):
                r = r8 * DMA_UNROLL + u
                src = pl.multiple_of(idx_ref[0, 0, r] * rtl, rtl)
                pltpu.make_async_copy(ys_hbm.at[pl.ds(src, rtl)],
                                      ybuf.at[dst_slot, pl.ds(pl.multiple_of(r * rtl, rtl), rtl)],
                                      sem.at[dst_slot]).start(priority=u % 2)
            return carry
        lax.fori_loop(0, n_asg // DMA_UNROLL, body, 0)

    @pl.when(i == 0)
    def _():
        start_gather(idxc_ref, 0)

    @pl.when(i + 1 < n)
    def _():
        start_gather(idxn_ref, 1 - slot)

    pltpu.make_async_copy(ys_hbm.at[pl.ds(0, n_asg * rtl)], ybuf.at[slot], sem.at[slot]).wait()

    g1, g2 = rt[:, 0:1], rt[:, 1:2]
    moe = jnp.concatenate(
        [g1 * ybuf[slot, pl.ds(c, tm, stride=2 * rtl), :]
         + g2 * ybuf[slot, pl.ds(rtl + c, tm, stride=2 * rtl), :] for c in range(rtl)], axis=1)
    x2 = x1_ref[...] + mod_ref[0][5:6] * moe
    modf = modf_ref[0]
    inv = lax.rsqrt(jnp.mean(x2 * x2, axis=-1, keepdims=True) + EPS)
    o_ref[...] = (x2 * inv) * (g_ref[...] * (1.0 + modf[1:2])) + modf[0:1]


def _final(x1, ys, slot_of, rt, mod, modf, boff, seq, lnf_g):
    t_tok = x1.shape[0]
    tm = TM_FIN
    n_steps = t_tok // tm
    per_seq = seq // tm
    tok = lambda i: (i, 0)
    idx3 = slot_of.reshape(n_steps, 1, 2 * tm)
    smem_blk = lambda fn: pl.BlockSpec((1, 1, 2 * tm), fn, memory_space=pltpu.SMEM)
    return pl.pallas_call(
        _final_kernel,
        grid=(n_steps,),
        in_specs=[smem_blk(lambda i: (i, 0, 0)),
                  smem_blk(lambda i: (jnp.minimum(i + 1, n_steps - 1), 0, 0)),
                  pl.BlockSpec((tm, D_MODEL), tok),
                  pl.BlockSpec(memory_space=pl.ANY),
                  pl.BlockSpec((tm, LANES), tok),
                  pl.BlockSpec((1, 6, D_MODEL), lambda i: (boff + i // per_seq, 0, 0)),
                  pl.BlockSpec((1, 2, D_MODEL), lambda i: (boff + i // per_seq, 0, 0)),
                  pl.BlockSpec((1, D_MODEL), lambda i: (0, 0))],
        out_specs=pl.BlockSpec((tm, D_MODEL), tok),
        out_shape=jax.ShapeDtypeStruct((t_tok, D_MODEL), F32),
        scratch_shapes=[pltpu.VMEM((2, tm * 2 * ROW_TILES, LANES), F32), pltpu.SemaphoreType.DMA((2,))],
        compiler_params=_cparams(("arbitrary",)),
        name="combine_final",
    )(idx3, idx3, x1, ys, rt, mod, modf, lnf_g.reshape(1, D_MODEL))


def _trunk(x, boff, mod, modf, prm):
    batch, seq, _ = x.shape
    x2d = x.reshape(batch * seq, D_MODEL)
    ta, tb = _rope_tables(seq)
    qa, ka, va, qb8, kb, vb = _inproj(x2d, mod, boff, seq, prm["ln1_g"], prm["w_in"], ta, tb,
                                      prm["qg"], prm["kg"], prm["bd"])
    oa = _attn_a(qa, ka, va, batch, seq)
    ob = _attn_b(qb8, kb, vb, batch, seq)
    x1, h2, rt, cnt = _outproj(x2d, oa, ob, mod, boff, seq, prm["on_a_g"], prm["on_b_g"], prm["w_out"],
                               prm["ln2_g"], prm["wr_split"], prm["br"], prm["ltri"])
    dest, block_e, n_used, fill_at, has_fill = _moe_plan(rt, cnt, TM_PROJ, BM_MOE)
    xs = _dispatch(h2, dest, block_e.shape[0] * BM_MOE, fill_at, has_fill, n_used)
    ys = _moe(xs, block_e, n_used, prm["w1"], prm["w3"], prm["w2"])
    y = _final(x1, ys, dest, rt, mod, modf, boff, seq, prm["lnf_g"])
    return y.reshape(batch, seq, D_MODEL)


def kernel(x_prompt, x_sample, c_prompt, c_sample, ln1_g, ln2_g, w_ada, b_ada, w_in, w_out, qn_g, kn_g,
           on_a_g, on_b_g, w_rg, b_rg, w_re, b_re, w1, w3, w2, lnf_g, w_adaf, b_adaf):
    assert ln1_g.shape[0] == 1, "single-layer trunk"
    nb_p, nb_s = c_prompt.shape[0], c_sample.shape[0]
    rows = -(-(nb_p + nb_s) // 8) * 8
    c_all = jnp.concatenate([c_prompt, c_sample, jnp.zeros((rows - nb_p - nb_s, D_MODEL), F32)], axis=0)
    mod = _modulation(c_all, w_ada[0], b_ada[0]).reshape(rows, 6, D_MODEL)
    modf = _modulation(c_all, w_adaf, b_adaf).reshape(rows, 2, D_MODEL)

    wr = jnp.concatenate([w_rg[0], w_re[0], jnp.zeros((D_MODEL, LANES - N_GROUPS - N_EXPERTS), F32)], axis=1)
    wrh = wr.astype(BF16)
    wrl = (wr - wrh.astype(F32)).astype(BF16)
    br = jnp.concatenate([b_rg[0], b_re[0], jnp.zeros((LANES - N_GROUPS - N_EXPERTS,), F32)]).reshape(1, LANES)
    seg = lax.broadcasted_iota(I32, (2 * LANES, LANES), 0) % LANES // HEAD_DIM
    bd = (seg == lax.broadcasted_iota(I32, (2 * LANES, LANES), 1) // HEAD_DIM).astype(BF16)
    ltri = (lax.broadcasted_iota(I32, (TM_PROJ, TM_PROJ), 1)
            < lax.broadcasted_iota(I32, (TM_PROJ, TM_PROJ), 0)).astype(BF16)
    prm = dict(
        ltri=ltri,
        ln1_g=ln1_g[0], ln2_g=ln2_g[0], w_in=w_in[0].astype(BF16), w_out=w_out[0].astype(BF16),
        qg=jnp.tile(qn_g[0], 2).reshape(1, LANES), kg=jnp.tile(kn_g[0], 2).reshape(1, LANES), bd=bd,
        on_a_g=on_a_g[0], on_b_g=on_b_g[0], wr_split=jnp.concatenate([wrh, wrl], axis=1), br=br,
        w1=w1[0].astype(BF16), w3=w3[0].astype(BF16), w2=w2[0].astype(BF16), lnf_g=lnf_g,
    )
    y_prompt = _trunk(x_prompt, 0, mod, modf, prm)
    y_sample = _trunk(x_sample, nb_p, mod, modf, prm)
    return (y_prompt, y_sample)
```

```python
import functools

import jax
import jax.numpy as jnp
from jax import lax
from jax.experimental import pallas as pl
from jax.experimental.pallas import tpu as pltpu

F32 = jnp.float32
BF16 = jnp.bfloat16
I32 = jnp.int32

D_MODEL = 1024
HEAD_DIM = 64
N_HEADS = 8
WIDTH = N_HEADS * HEAD_DIM
KV_WIDTH = 2 * HEAD_DIM
IN_COLS = 3 * WIDTH + WIDTH + 2 * KV_WIDTH
ROPE_THETA = 500000.0
ROPE_DIMS_A = 16
AXIAL_THETA = 10000.0
GRID_W = 64
N_GROUPS = 4
EXPERTS_PER_GROUP = 8
N_EXPERTS = 32
D_EXPERT = 512
EPS = 1e-6
NEG = -1e30
DIL_W = 64
A_BATCH = 8

LANES = 128
VMEM_LIMIT = 56 * 1024 * 1024

TM_PROJ = 512
TILE_A = 1024
TQ_B = 256
NQ_B = 4
B_ITEMS_PER_TRIP = 8
BF16_ROWS = 16
LOG2E = 1.4426950408889634
ROW_TILES = D_MODEL // LANES
BM_MOE = 512
DMA_UNROLL = 8
TM_FIN = 512


def _cparams(sem):
    return pltpu.CompilerParams(dimension_semantics=sem, vmem_limit_bytes=VMEM_LIMIT)


def _mod_kernel(c_ref, w_ref, b_ref, o_ref):
    c = c_ref[...]
    a = c / (1.0 + jnp.exp(-c))
    o_ref[...] = jnp.dot(a, w_ref[...], preferred_element_type=F32,
                         precision=lax.Precision.HIGHEST) + b_ref[...]


def _modulation(c_pad, w, b):
    rows, d = c_pad.shape
    n = w.shape[1]
    tn = 1024
    return pl.pallas_call(
        _mod_kernel,
        grid=(n // tn,),
        in_specs=[pl.BlockSpec((rows, d), lambda j: (0, 0)),
                  pl.BlockSpec((d, tn), lambda j: (0, j)),
                  pl.BlockSpec((1, tn), lambda j: (0, j))],
        out_specs=pl.BlockSpec((rows, tn), lambda j: (0, j)),
        out_shape=jax.ShapeDtypeStruct((rows, n), F32),
        compiler_params=_cparams(("arbitrary",)),
        name="modulation",
    )(c_pad, w, b.reshape(1, n))


def _rope_tables(seq):
    t = jnp.arange(seq)
    tf = t.astype(F32)
    inv_a = ROPE_THETA ** (-jnp.arange(0, ROPE_DIMS_A, 2, dtype=F32) / ROPE_DIMS_A)
    ang = tf[:, None] * inv_a[None, :]
    cos, sin = jnp.cos(ang), jnp.sin(ang)
    rest = HEAD_DIM - ROPE_DIMS_A
    one = jnp.ones((seq, rest), F32)
    zero = jnp.zeros((seq, rest), F32)
    z8 = jnp.zeros_like(sin)
    cos64 = jnp.concatenate([cos, cos, one], 1)
    sm64 = jnp.concatenate([-sin, z8, zero], 1)
    sp64 = jnp.concatenate([z8, sin, zero], 1)
    ta = jnp.stack([jnp.tile(a, (1, 2)) for a in (cos64, sm64, sp64)])

    n_ax = HEAD_DIM // 2
    inv_b = AXIAL_THETA ** (-jnp.arange(0, n_ax, 2, dtype=F32) / n_ax)
    row_pos = (t // GRID_W).astype(F32)
    col_pos = (t % GRID_W).astype(F32)
    ar = row_pos[:, None] * inv_b[None, :]
    ac = col_pos[:, None] * inv_b[None, :]
    z16 = jnp.zeros_like(ar)
    cos64 = jnp.concatenate([jnp.cos(ar), jnp.cos(ar), jnp.cos(ac), jnp.cos(ac)], 1)
    sm64 = jnp.concatenate([-jnp.sin(ar), z16, -jnp.sin(ac), z16], 1)
    sp64 = jnp.concatenate([z16, jnp.sin(ar), z16, jnp.sin(ac)], 1)
    tb = jnp.stack([jnp.tile(a, (1, 2)) for a in (cos64, sm64, sp64)])
    return ta, tb


def _inproj_kernel(x_ref, mod_ref, g_ref, w_ref, ta_ref, tb_ref, qg_ref, kg_ref, bd_ref,
                   qa_ref, ka_ref, va_ref, qb_ref, kb_ref, vb_ref):
    x = x_ref[...]
    mod = mod_ref[0]
    inv = lax.rsqrt(jnp.mean(x * x, axis=-1, keepdims=True) + EPS)
    h = (x * inv) * (g_ref[...] * (1.0 + mod[1:2])) + mod[0:1]
    hb = h.astype(BF16)
    cos_a, sm_a, sp_a = ta_ref[0], ta_ref[1], ta_ref[2]
    cos_b, sm_b, sp_b = tb_ref[0], tb_ref[1], tb_ref[2]
    bd = bd_ref[...]
    lane = lax.broadcasted_iota(I32, (1, LANES), 1)
    low = lane < HEAD_DIM

    def proj(lo, n):
        return jnp.dot(hb, w_ref[:, lo:lo + n], preferred_element_type=F32)

    def rope(y, cos, sm, sp, sh):
        return y * cos + pltpu.roll(y, LANES - sh, 1) * sm + pltpu.roll(y, sh, 1) * sp

    def headnorm(y, g):
        t = y * y
        thi = t.astype(BF16)
        tlo = (t - thi.astype(F32)).astype(BF16)
        ss = jnp.dot(jnp.concatenate([thi, tlo], axis=1), bd, preferred_element_type=F32)
        return y * lax.rsqrt(ss * (1.0 / HEAD_DIM) + EPS) * g

    scale = HEAD_DIM ** -0.5
    qa = proj(0, WIDTH)
    ka = proj(WIDTH, WIDTH)
    for c in range(WIDTH // LANES):
        sl = slice(c * LANES, (c + 1) * LANES)
        qa_ref[:, sl] = rope(qa[:, sl], cos_a, sm_a, sp_a, 8) * (scale * LOG2E)
        ka_ref[:, sl] = rope(ka[:, sl], cos_a, sm_a, sp_a, 8)
    va_ref[...] = proj(2 * WIDTH, WIDTH)

    qb = proj(3 * WIDTH, WIDTH)
    qg = qg_ref[...]
    for c in range(WIDTH // LANES):
        y = rope(headnorm(qb[:, c * LANES:(c + 1) * LANES], qg), cos_b, sm_b, sp_b, 16) * (scale * LOG2E)
        ysw = pltpu.roll(y, HEAD_DIM, 1)
        grp = c // 2
        h0 = y if grp == 0 else ysw
        h1 = ysw if grp == 0 else y
        keep = low if grp == 0 else jnp.logical_not(low)
        for u, hh in enumerate((h0, h1)):
            ht = jnp.where(keep, hh, 0.0).T.astype(BF16)
            for sub in range(ht.shape[1] // TQ_B):
                qb_ref[sub, 2 * c + u] = ht[:, sub * TQ_B:(sub + 1) * TQ_B]
    kb = proj(4 * WIDTH, KV_WIDTH)
    kb_ref[...] = rope(headnorm(kb, kg_ref[...]), cos_b, sm_b, sp_b, 16).astype(BF16)
    vb_ref[0] = proj(4 * WIDTH + KV_WIDTH, KV_WIDTH).T.astype(BF16)


def _inproj(x2d, mod, boff, seq, ln1_g, w_in_bf, ta, tb, qg, kg, bd):
    t_tok = x2d.shape[0]
    tm = TM_PROJ
    per_seq = seq // tm
    tok = lambda i: (i, 0)
    const = lambda i: (0, 0)
    return pl.pallas_call(
        _inproj_kernel,
        grid=(t_tok // tm,),
        in_specs=[
            pl.BlockSpec((tm, D_MODEL), tok),
            pl.BlockSpec((1, 6, D_MODEL), lambda i: (boff + i // per_seq, 0, 0)),
            pl.BlockSpec((1, D_MODEL), const),
            pl.BlockSpec((D_MODEL, IN_COLS), const),
            pl.BlockSpec((3, tm, LANES), lambda i: (0, i % per_seq, 0)),
            pl.BlockSpec((3, tm, LANES), lambda i: (0, i % per_seq, 0)),
            pl.BlockSpec((1, LANES), const),
            pl.BlockSpec((1, LANES), const),
            pl.BlockSpec((2 * LANES, LANES), const),
        ],
        out_specs=[
            pl.BlockSpec((tm, WIDTH), tok),
            pl.BlockSpec((tm, WIDTH), tok),
            pl.BlockSpec((tm, WIDTH), tok),
            pl.BlockSpec((tm // TQ_B, N_HEADS, LANES, TQ_B), lambda i: (i, 0, 0, 0)),
            pl.BlockSpec((tm, KV_WIDTH), tok),
            pl.BlockSpec((1, KV_WIDTH, tm), lambda i: (i, 0, 0)),
        ],
        out_shape=[
            jax.ShapeDtypeStruct((t_tok, WIDTH), F32),
            jax.ShapeDtypeStruct((t_tok, WIDTH), F32),
            jax.ShapeDtypeStruct((t_tok, WIDTH), F32),
            jax.ShapeDtypeStruct((t_tok // TQ_B, N_HEADS, LANES, TQ_B), BF16),
            jax.ShapeDtypeStruct((t_tok, KV_WIDTH), BF16),
            jax.ShapeDtypeStruct((t_tok // tm, KV_WIDTH, tm), BF16),
        ],
        compiler_params=_cparams(("parallel",)),
        name="inproj",
    )(x2d, mod, ln1_g.reshape(1, D_MODEL), w_in_bf, ta, tb, qg, kg, bd)


def _attn_a_kernel(q_ref, kp_ref, kc_ref, kn_ref, vp_ref, vc_ref, vn_ref, o_ref,
                   m1, l1, a1, m2, l2, a2, m3, l3, a3, *, seq):
    n = pl.program_id(1)
    t = TILE_A
    w = DIL_W

    lane = lax.broadcasted_iota(I32, (1, LANES), 1)
    low = lane < HEAD_DIM
    row_i = lax.broadcasted_iota(I32, (2 * w, 1), 0) & (w - 1)
    col_m = lax.broadcasted_iota(I32, (1, 3 * w), 1)
    band_bias = jnp.where(jnp.abs(col_m - w - row_i) <= w, 0.0, NEG)
    base = n * t - t

    def window_rows(refs, start, count, d):
        parts = []
        while count > 0:
            ref = refs[start // t]
            local = start % t
            take = min(count, -(-(t - local) // d))
            parts.append(ref[pl.ds(local, take, stride=d), :] if d > 1 else ref[pl.ds(local, take), :])
            start += take * d
            count -= take
        return parts[0] if len(parts) == 1 else jnp.concatenate(parts, axis=0)

    units = []
    for d, m_s, l_s, a_s in ((1, m1, l1, a1), (4, m2, l2, a2), (16, m3, l3, a3)):
        for j in range(t // w):
            qs = (j // d) * (w * d) + (j % d)
            units.append((d, qs, m_s, l_s, a_s))

    def rows(qs, d):
        return pl.ds(qs, w, stride=d) if d > 1 else pl.ds(qs, w)

    def score(unit):
        d, qs = unit[0], unit[1]
        ks = qs + t - w * d
        q = q_ref[rows(qs, d), :]
        k = window_rows((kp_ref, kc_ref, kn_ref), ks, 3 * w, d).astype(BF16)
        q2 = jnp.concatenate([jnp.where(low, q, 0.0), jnp.where(low, 0.0, q)], axis=0).astype(BF16)
        s = lax.dot_general(q2, k, (((1,), (1,)), ((), ())), preferred_element_type=F32) + band_bias
        if ks >= t and ks + (3 * w - 1) * d < 2 * t:
            return s
        kpos = base + ks + d * col_m
        edge_bias = jnp.where(kpos >= 0, jnp.where(kpos < seq, 0.0, NEG), NEG)
        return s + edge_bias

    def finish(unit, s):
        d, qs, m_s, l_s, a_s = unit
        ks = qs + t - w * d
        v = window_rows((vp_ref, vc_ref, vn_ref), ks, 3 * w, d).astype(BF16)
        mx = jnp.max(s, axis=-1, keepdims=True)
        p = jnp.exp2(s - mx)
        den = jnp.sum(p, axis=-1, keepdims=True)
        o = jnp.dot(p.astype(BF16), v, preferred_element_type=F32)
        m_s[rows(qs, d), :] = jnp.where(low, mx[0:w], mx[w:2 * w])
        l_s[rows(qs, d), :] = jnp.where(low, den[0:w], den[w:2 * w])
        a_s[rows(qs, d), :] = jnp.where(low, o[0:w], o[w:2 * w])

    nb = A_BATCH
    pending = [score(u) for u in units[:nb]]
    for b0 in range(0, len(units), nb):
        ahead = [score(u) for u in units[b0 + nb:b0 + 2 * nb]]
        for u, s in zip(units[b0:b0 + nb], pending):
            finish(u, s)
        pending = ahead

    mm = jnp.maximum(jnp.maximum(m1[...], m2[...]), m3[...])
    w1 = jnp.exp2(m1[...] - mm)
    w2 = jnp.exp2(m2[...] - mm)
    w3 = jnp.exp2(m3[...] - mm)
    num = w1 * a1[...] + w2 * a2[...] + w3 * a3[...]
    den = w1 * l1[...] + w2 * l2[...] + w3 * l3[...]
    o_ref[...] = (num / den).astype(o_ref.dtype)


def _attn_a(qa, ka, va, batch, seq):
    t = TILE_A
    nt = seq // t
    q3 = qa.reshape(batch, seq, WIDTH)
    k3 = ka.reshape(batch, seq, WIDTH)
    v3 = va.reshape(batch, seq, WIDTH)
    cur = lambda b, n, p: (b, n, p)
    prev = lambda b, n, p: (b, jnp.maximum(n - 1, 0), p)
    nxt = lambda b, n, p: (b, jnp.minimum(n + 1, nt - 1), p)
    blk = (None, t, LANES)
    scr = pltpu.VMEM((t, LANES), F32)
    out = pl.pallas_call(
        functools.partial(_attn_a_kernel, seq=seq),
        grid=(batch, nt, WIDTH // LANES),
        in_specs=[pl.BlockSpec(blk, cur),
                  pl.BlockSpec(blk, prev), pl.BlockSpec(blk, cur), pl.BlockSpec(blk, nxt),
                  pl.BlockSpec(blk, prev), pl.BlockSpec(blk, cur), pl.BlockSpec(blk, nxt)],
        out_specs=pl.BlockSpec(blk, cur),
        out_shape=jax.ShapeDtypeStruct((batch, seq, WIDTH), BF16),
        scratch_shapes=[scr] * 9,
        compiler_params=_cparams(("parallel", "parallel", "parallel")),
        name="attn_dilated",
    )(q3, k3, k3, k3, v3, v3, v3)
    return out.reshape(batch * seq, WIDTH)


def _attn_b_kernel(qt_ref, k_ref, vt_ref, o_ref, m_s, acc_s, s_a, s_b, *, nkv):
    nqt, tq = qt_ref.shape[0], qt_ref.shape[3]
    tk = k_ref.shape[1]
    m_s[...] = jnp.full(m_s.shape, NEG, F32)
    acc_s[...] = jnp.zeros(acc_s.shape, F32)
    fold = 8
    ones_rows = jnp.ones((BF16_ROWS, tk), BF16)
    log_nkv = nkv.bit_length() - 1

    def score(t, dst, h):
        dst[h] = jnp.dot(k_ref[t & (nkv - 1)], qt_ref[t >> log_nkv, h], preferred_element_type=F32)

    def update(t, src, h):
        qi, j = t >> log_nkv, t & (nkv - 1)
        cs = slice(tq * h, tq * (h + 1))
        r0 = HEAD_DIM * (h // (N_HEADS // 2))
        vt = jnp.concatenate([vt_ref[j, r0:r0 + HEAD_DIM, :], ones_rows], axis=0)
        s = src[h]
        part = jnp.max(s.reshape(fold, tk // fold, tq), axis=0)
        m_old = m_s[qi, :, cs]
        m_new = jnp.maximum(m_old, jnp.max(part, axis=0, keepdims=True))
        alpha = jnp.exp2(m_old - m_new)
        p = jnp.exp2((s - m_new).astype(BF16))
        acc_s[qi, :, cs] = alpha * acc_s[qi, :, cs] + jnp.dot(vt, p, preferred_element_type=F32)
        m_s[qi, :, cs] = m_new

    def step(t_next, dst, t_cur, src):
        for h in range(N_HEADS):
            if t_next is not None:
                score(t_next, dst, h)
            update(t_cur, src, h)

    n_items = nqt * nkv
    for h in range(N_HEADS):
        score(0, s_a, h)

    bufs = (s_a, s_b)
    per_trip = B_ITEMS_PER_TRIP

    def body(trip, carry):
        t0 = per_trip * trip
        for u in range(per_trip):
            step(t0 + u + 1, bufs[(u + 1) % 2], t0 + u, bufs[u % 2])
        return carry

    lax.fori_loop(0, n_items // per_trip - 1, body, 0)
    t0 = n_items - per_trip
    for u in range(per_trip - 1):
        step(t0 + u + 1, bufs[(u + 1) % 2], t0 + u, bufs[u % 2])
    step(None, None, n_items - 1, bufs[(per_trip - 1) % 2])

    for qi in range(nqt):
        o = acc_s[qi, 0:HEAD_DIM] / acc_s[qi, HEAD_DIM:HEAD_DIM + 1]
        for c in range(N_HEADS // 2):
            pair = jnp.concatenate([o[:, (2 * c) * tq:(2 * c + 1) * tq],
                                    o[:, (2 * c + 1) * tq:(2 * c + 2) * tq]], axis=0)
            o_ref[qi * tq:(qi + 1) * tq, c * LANES:(c + 1) * LANES] = pair.T.astype(o_ref.dtype)


def _attn_b(qbt, kb, vbt, batch, seq):
    tq, tk, nqt = TQ_B, TM_PROJ, NQ_B
    steps = seq // (tq * nqt)
    nkv = seq // tk
    assert nkv & (nkv - 1) == 0 and nkv >= 2
    sbuf = pltpu.VMEM((N_HEADS, tk, tq), F32)
    k3 = kb.reshape(batch * nkv, tk, KV_WIDTH)
    return pl.pallas_call(
        functools.partial(_attn_b_kernel, nkv=nkv),
        grid=(batch, steps),
        in_specs=[pl.BlockSpec((nqt, N_HEADS, LANES, tq), lambda b, i: (b * steps + i, 0, 0, 0)),
                  pl.BlockSpec((nkv, tk, KV_WIDTH), lambda b, i: (b, 0, 0)),
                  pl.BlockSpec((nkv, KV_WIDTH, tk), lambda b, i: (b, 0, 0))],
        out_specs=pl.BlockSpec((nqt * tq, WIDTH), lambda b, i: (b * steps + i, 0)),
        out_shape=jax.ShapeDtypeStruct((batch * seq, WIDTH), BF16),
        scratch_shapes=[pltpu.VMEM((nqt, 1, N_HEADS * tq), F32),
                        pltpu.VMEM((nqt, HEAD_DIM + BF16_ROWS, N_HEADS * tq), F32), sbuf, sbuf],
        compiler_params=_cparams(("parallel", "parallel")),
        name="attn_dense",
    )(qbt, k3, vbt)


def _outproj_kernel(x_ref, oa_ref, ob_ref, mod_ref, ga_ref, gb_ref, wo_ref, ln2_ref,
                    wr_ref, br_ref, ltri_ref, x1_ref, h2_ref, rt_ref, cnt_ref):
    def rms(o, g):
        return o * lax.rsqrt(jnp.mean(o * o, axis=-1, keepdims=True) + EPS) * g

    oa = rms(oa_ref[...].astype(F32), ga_ref[...])
    ob = rms(ob_ref[...].astype(F32), gb_ref[...])
    o = jnp.concatenate([oa, ob], axis=1).astype(BF16)
    mod = mod_ref[0]
    x1 = x_ref[...] + mod[2:3] * jnp.dot(o, wo_ref[...], preferred_element_type=F32)
    x1_ref[...] = x1
    inv = lax.rsqrt(jnp.mean(x1 * x1, axis=-1, keepdims=True) + EPS)
    h2 = (x1 * inv) * (ln2_ref[...] * (1.0 + mod[4:5])) + mod[3:4]
    tm = h2.shape[0]
    for c in range(ROW_TILES):
        h2_ref[pl.ds(c, tm, stride=ROW_TILES), :] = h2[:, c * LANES:(c + 1) * LANES]

    hh = h2.astype(BF16)
    hl = (h2 - hh.astype(F32)).astype(BF16)
    both = jnp.dot(hh, wr_ref[...], preferred_element_type=F32)
    logits = (both[:, 0:LANES] + both[:, LANES:2 * LANES]
              + jnp.dot(hl, wr_ref[:, 0:LANES], preferred_element_type=F32)) + br_ref[...]

    lane = lax.broadcasted_iota(I32, logits.shape, 1).astype(F32)
    big = jnp.float32(LANES)
    is_g = lane < N_GROUPS
    gl = jnp.where(is_g, logits, -jnp.inf)
    gmax = jnp.max(gl, axis=-1, keepdims=True)
    g_sel = jnp.min(jnp.where(gl == gmax, lane, big), axis=-1, keepdims=True)
    g_den = jnp.sum(jnp.exp(gl - gmax), axis=-1, keepdims=True)
    g_w = 1.0 / g_den
    e_lo = N_GROUPS + EXPERTS_PER_GROUP * g_sel
    is_e = (lane >= e_lo) & (lane < e_lo + EXPERTS_PER_GROUP)
    el = jnp.where(is_e, logits, -jnp.inf)
    emax = jnp.max(el, axis=-1, keepdims=True)
    p = jnp.where(is_e, jnp.exp(el - emax), -1.0)
    p1 = jnp.max(p, axis=-1, keepdims=True)
    i1 = jnp.min(jnp.where(p == p1, lane, big), axis=-1, keepdims=True)
    pm = jnp.where(lane == i1, -1.0, p)
    p2 = jnp.max(pm, axis=-1, keepdims=True)
    i2 = jnp.min(jnp.where(pm == p2, lane, big), axis=-1, keepdims=True)
    tot = p1 + p2
    gate1 = g_w * p1 / tot
    gate2 = g_w * p2 / tot
    e1 = i1 - N_GROUPS
    e2 = i2 - N_GROUPS
    memb = jnp.where(lane == e1, 1.0, jnp.where(lane == e2, 1.0, 0.0))
    before = jnp.dot(ltri_ref[...], memb.astype(BF16), preferred_element_type=F32)
    r1 = jnp.sum(jnp.where(lane == e1, before, 0.0), axis=-1, keepdims=True)
    r2 = jnp.sum(jnp.where(lane == e2, before, 0.0), axis=-1, keepdims=True)
    cnt_ref[0] = jnp.broadcast_to(jnp.sum(memb, axis=0, keepdims=True), cnt_ref.shape[1:])
    cols = (gate1, gate2, e1, e2, r1, r2)
    out = jnp.zeros_like(logits)
    for idx, col in enumerate(cols):
        out = jnp.where(lane == idx, col, out)
    rt_ref[...] = out


def _outproj(x2d, oa, ob, mod, boff, seq, ga, gb, wo_bf, ln2_g, wr_split, br, ltri):
    t_tok = x2d.shape[0]
    tm = TM_PROJ
    per_seq = seq // tm
    tok = lambda i: (i, 0)
    const = lambda i: (0, 0)
    return pl.pallas_call(
        _outproj_kernel,
        grid=(t_tok // tm,),
        in_specs=[
            pl.BlockSpec((tm, D_MODEL), tok),
            pl.BlockSpec((tm, WIDTH), tok),
            pl.BlockSpec((tm, WIDTH), tok),
            pl.BlockSpec((1, 6, D_MODEL), lambda i: (boff + i // per_seq, 0, 0)),
            pl.BlockSpec((1, WIDTH), const),
            pl.BlockSpec((1, WIDTH), const),
            pl.BlockSpec((D_MODEL, D_MODEL), const),
            pl.BlockSpec((1, D_MODEL), const),
            pl.BlockSpec((D_MODEL, 2 * LANES), const),
            pl.BlockSpec((1, LANES), const),
            pl.BlockSpec((tm, tm), const),
        ],
        out_specs=[pl.BlockSpec((tm, D_MODEL), tok),
                   pl.BlockSpec((tm * ROW_TILES, LANES), tok),
                   pl.BlockSpec((tm, LANES), tok),
                   pl.BlockSpec((1, 8, LANES), lambda i: (i, 0, 0))],
        out_shape=[jax.ShapeDtypeStruct((t_tok, D_MODEL), F32),
                   jax.ShapeDtypeStruct((t_tok * ROW_TILES, LANES), F32),
                   jax.ShapeDtypeStruct((t_tok, LANES), F32),
                   jax.ShapeDtypeStruct((t_tok // tm, 8, LANES), F32)],
        compiler_params=_cparams(("parallel",)),
        name="outproj_router",
    )(x2d, oa, ob, mod, ga.reshape(1, WIDTH), gb.reshape(1, WIDTH), wo_bf,
      ln2_g.reshape(1, D_MODEL), wr_split, br, ltri)


def _dispatch_kernel(fill_ref, has_ref, nu_ref, idx_ref, h2_ref, out_hbm, zbuf, sem, fsem):
    i = pl.program_id(0)
    n_asg = idx_ref.shape[2]
    rt = ROW_TILES

    def fill_copy(e):
        start = pl.multiple_of(fill_ref[e] * rt, rt)
        return pltpu.make_async_copy(zbuf, out_hbm.at[pl.ds(start, zbuf.shape[0])], fsem)

    def tail_copy(b):
        start = pl.multiple_of(b * zbuf.shape[0], zbuf.shape[0])
        return pltpu.make_async_copy(zbuf, out_hbm.at[pl.ds(start, zbuf.shape[0])], fsem)

    @pl.when(i == 0)
    def _():
        zbuf[...] = jnp.zeros(zbuf.shape, F32)
        for e in range(N_EXPERTS):
            @pl.when(has_ref[e] > 0)
            def _():
                fill_copy(e).start()
        for e in range(N_EXPERTS):
            @pl.when(has_ref[e] > 0)
            def _():
                fill_copy(e).wait()
        n_blocks = out_hbm.shape[0] // zbuf.shape[0]

        def tail(b, carry):
            tail_copy(b).start()
            tail_copy(b).wait()
            return carry
        lax.fori_loop(nu_ref[0], n_blocks, tail, 0)

    trips = n_asg // DMA_UNROLL
    for half in range(2):
        def body(r8, carry, half=half):
            for u in range(DMA_UNROLL):
                r = r8 * DMA_UNROLL + u
                src = pl.multiple_of((r >> 1) * rt, rt)
                dst = pl.multiple_of(idx_ref[0, 0, r] * rt, rt)
                pltpu.make_async_copy(h2_ref.at[pl.ds(src, rt)], out_hbm.at[pl.ds(dst, rt)],
                                      sem.at[half]).start(priority=u % 2)
            return carry
        lax.fori_loop(half * (trips // 2), (half + 1) * (trips // 2), body, 0)
    for half in range(2):
        pltpu.make_async_copy(h2_ref, out_hbm.at[pl.ds(0, h2_ref.shape[0])], sem.at[half]).wait()


def _dispatch(h2, slot_of, n_slots, fill_at, has_fill, n_used):
    tm = TM_PROJ
    n_steps = h2.shape[0] // (tm * ROW_TILES)
    grid_spec = pltpu.PrefetchScalarGridSpec(
        num_scalar_prefetch=3,
        grid=(n_steps,),
        in_specs=[pl.BlockSpec((1, 1, 2 * tm), lambda i, fa, hf, nu: (i, 0, 0), memory_space=pltpu.SMEM),
                  pl.BlockSpec((tm * ROW_TILES, LANES), lambda i, fa, hf, nu: (i, 0))],
        out_specs=pl.BlockSpec(memory_space=pl.ANY),
        scratch_shapes=[pltpu.VMEM((BM_MOE * ROW_TILES, LANES), F32),
                        pltpu.SemaphoreType.DMA((2,)), pltpu.SemaphoreType.DMA(())],
    )
    return pl.pallas_call(
        _dispatch_kernel,
        grid_spec=grid_spec,
        out_shape=jax.ShapeDtypeStruct((n_slots * ROW_TILES, LANES), F32),
        compiler_params=_cparams(("arbitrary",)),
        name="moe_dispatch",
    )(fill_at, has_fill, n_used, slot_of.reshape(n_steps, 1, 2 * tm), h2)


def _moe_kernel(be_ref, nu_ref, x_ref, w1_ref, w3_ref, w2_ref, y_ref):
    del be_ref
    bm = BM_MOE
    rt = ROW_TILES

    @pl.when(pl.program_id(0) < nu_ref[0])
    def _():
        x = jnp.concatenate([x_ref[pl.ds(c, bm, stride=rt), :] for c in range(rt)], axis=1).astype(BF16)
        a = jnp.dot(x, w1_ref[...], preferred_element_type=F32)
        b = jnp.dot(x, w3_ref[...], preferred_element_type=F32)
        hmid = ((a / (1.0 + jnp.exp(-a))) * b).astype(BF16)
        y = jnp.dot(hmid, w2_ref[...], preferred_element_type=F32)
        for c in range(rt):
            y_ref[pl.ds(c, bm, stride=rt), :] = y[:, c * LANES:(c + 1) * LANES]

    @pl.when(pl.program_id(0) >= nu_ref[0])
    def _():
        y_ref[...] = jnp.zeros(y_ref.shape, F32)


def _moe(xs, block_e, n_used, w1_bf, w3_bf, w2_bf):
    bm = BM_MOE
    n_blocks = block_e.shape[0]
    row_in = pl.BlockSpec((bm * ROW_TILES, LANES), lambda i, be, nu: (jnp.minimum(i, nu[0] - 1), 0))
    row_out = pl.BlockSpec((bm * ROW_TILES, LANES), lambda i, be, nu: (i, 0))
    grid_spec = pltpu.PrefetchScalarGridSpec(
        num_scalar_prefetch=2,
        grid=(n_blocks,),
        in_specs=[
            row_in,
            pl.BlockSpec((None, D_MODEL, D_EXPERT), lambda i, be, nu: (be[i], 0, 0)),
            pl.BlockSpec((None, D_MODEL, D_EXPERT), lambda i, be, nu: (be[i], 0, 0)),
            pl.BlockSpec((None, D_EXPERT, D_MODEL), lambda i, be, nu: (be[i], 0, 0)),
        ],
        out_specs=row_out,
    )
    return pl.pallas_call(
        _moe_kernel,
        grid_spec=grid_spec,
        out_shape=jax.ShapeDtypeStruct(xs.shape, F32),
        compiler_params=_cparams(("arbitrary",)),
        name="moe_experts",
    )(block_e, n_used, xs, w1_bf, w3_bf, w2_bf)


def _moe_plan(rt, cnt, tm, bm):
    t_tok = rt.shape[0]
    expert = rt[:, 2:4].astype(I32)
    rank_in_tile = rt[:, 4:6].astype(I32)
    tile_cnt = cnt[:, 0, :N_EXPERTS].astype(I32)
    tile_off = jnp.cumsum(tile_cnt, axis=0) - tile_cnt
    counts = jnp.sum(tile_cnt, axis=0)
    pcounts = (counts + bm - 1) // bm * bm
    pends = jnp.cumsum(pcounts)
    pstarts = pends - pcounts
    base = jnp.repeat(pstarts[None, :] + tile_off, tm, axis=0)
    onehot = expert[:, :, None] == jnp.arange(N_EXPERTS, dtype=I32)[None, None, :]
    dest = jnp.sum(jnp.where(onehot, base[:, None, :], 0), axis=-1) + rank_in_tile
    n_blocks = 2 * t_tok // bm + N_EXPERTS
    blk_start = jnp.arange(n_blocks, dtype=I32) * bm
    block_e = jnp.minimum(jnp.sum((pends[None, :] <= blk_start[:, None]).astype(I32), axis=1), N_EXPERTS - 1)
    n_used = (pends[-1] // bm).astype(I32).reshape(1)
    fill_at = jnp.maximum(pends - bm, 0).astype(I32)
    has_fill = (pcounts > 0).astype(I32)
    return dest.reshape(-1).astype(I32), block_e.astype(I32), n_used, fill_at, has_fill


def _final_kernel(idxc_ref, idxn_ref, x1_ref, ys_hbm, rt_ref, mod_ref, modf_ref, g_ref, o_ref, ybuf, sem):
    i = pl.program_id(0)
    n = pl.num_programs(0)
    slot = i % 2
    rt = rt_ref[...]
    tm = rt.shape[0]
    n_asg = 2 * tm
    rtl = ROW_TILES

    def start_gather(idx_ref, dst_slot):
        def body(r8, carry):
            for u in range(DMA_UNROLL):
                r = r8 * DMA_UNROLL + u
                src = pl.multiple_of(idx_ref[0, 0, r] * rtl, rtl)
                pltpu.make_async_copy(ys_hbm.at[pl.ds(src, rtl)],
                                      ybuf.at[dst_slot, pl.ds(pl.multiple_of(r * rtl, rtl), rtl)],
                                      sem.at[dst_slot]).start(priority=u % 2)
            return carry
        lax.fori_loop(0, n_asg // DMA_UNROLL, body, 0)

    @pl.when(i == 0)
    def _():
        start_gather(idxc_ref, 0)

    @pl.when(i + 1 < n)
    def _():
        start_gather(idxn_ref, 1 - slot)

    pltpu.make_async_copy(ys_hbm.at[pl.ds(0, n_asg * rtl)], ybuf.at[slot], sem.at[slot]).wait()

    g1, g2 = rt[:, 0:1], rt[:, 1:2]
    moe = jnp.concatenate(
        [g1 * ybuf[slot, pl.ds(c, tm, stride=2 * rtl), :]
         + g2 * ybuf[slot, pl.ds(rtl + c, tm, stride=2 * rtl), :] for c in range(rtl)], axis=1)
    x2 = x1_ref[...] + mod_ref[0][5:6] * moe
    modf = modf_ref[0]
    inv = lax.rsqrt(jnp.mean(x2 * x2, axis=-1, keepdims=True) + EPS)
    o_ref[...] = (x2 * inv) * (g_ref[...] * (1.0 + modf[1:2])) + modf[0:1]


def _final(x1, ys, slot_of, rt, mod, modf, boff, seq, lnf_g):
    t_tok = x1.shape[0]
    tm = TM_FIN
    n_steps = t_tok // tm
    per_seq = seq // tm
    tok = lambda i: (i, 0)
    idx3 = slot_of.reshape(n_steps, 1, 2 * tm)
    smem_blk = lambda fn: pl.BlockSpec((1, 1, 2 * tm), fn, memory_space=pltpu.SMEM)
    return pl.pallas_call(
        _final_kernel,
        grid=(n_steps,),
        in_specs=[smem_blk(lambda i: (i, 0, 0)),
                  smem_blk(lambda i: (jnp.minimum(i + 1, n_steps - 1), 0, 0)),
                  pl.BlockSpec((tm, D_MODEL), tok),
                  pl.BlockSpec(memory_space=pl.ANY),
                  pl.BlockSpec((tm, LANES), tok),
                  pl.BlockSpec((1, 6, D_MODEL), lambda i: (boff + i // per_seq, 0, 0)),
                  pl.BlockSpec((1, 2, D_MODEL), lambda i: (boff + i // per_seq, 0, 0)),
                  pl.BlockSpec((1, D_MODEL), lambda i: (0, 0))],
        out_specs=pl.BlockSpec((tm, D_MODEL), tok),
        out_shape=jax.ShapeDtypeStruct((t_tok, D_MODEL), F32),
        scratch_shapes=[pltpu.VMEM((2, tm * 2 * ROW_TILES, LANES), F32), pltpu.SemaphoreType.DMA((2,))],
        compiler_params=_cparams(("arbitrary",)),
        name="combine_final",
    )(idx3, idx3, x1, ys, rt, mod, modf, lnf_g.reshape(1, D_MODEL))


def _trunk(x, boff, mod, modf, prm):
    batch, seq, _ = x.shape
    x2d = x.reshape(batch * seq, D_MODEL)
    ta, tb = _rope_tables(seq)
    qa, ka, va, qb8, kb, vb = _inproj(x2d, mod, boff, seq, prm["ln1_g"], prm["w_in"], ta, tb,
                                      prm["qg"], prm["kg"], prm["bd"])
    oa = _attn_a(qa, ka, va, batch, seq)
    ob = _attn_b(qb8, kb, vb, batch, seq)
    x1, h2, rt, cnt = _outproj(x2d, oa, ob, mod, boff, seq, prm["on_a_g"], prm["on_b_g"], prm["w_out"],
                               prm["ln2_g"], prm["wr_split"], prm["br"], prm["ltri"])
    dest, block_e, n_used, fill_at, has_fill = _moe_plan(rt, cnt, TM_PROJ, BM_MOE)
    xs = _dispatch(h2, dest, block_e.shape[0] * BM_MOE, fill_at, has_fill, n_used)
    ys = _moe(xs, block_e, n_used, prm["w1"], prm["w3"], prm["w2"])
    y = _final(x1, ys, dest, rt, mod, modf, boff, seq, prm["lnf_g"])
    return y.reshape(batch, seq, D_MODEL)


def kernel(x_prompt, x_sample, c_prompt, c_sample, ln1_g, ln2_g, w_ada, b_ada, w_in, w_out, qn_g, kn_g,
           on_a_g, on_b_g, w_rg, b_rg, w_re, b_re, w1, w3, w2, lnf_g, w_adaf, b_adaf):
    assert ln1_g.shape[0] == 1, "single-layer trunk"
    nb_p, nb_s = c_prompt.shape[0], c_sample.shape[0]
    rows = -(-(nb_p + nb_s) // 8) * 8
    c_all = jnp.concatenate([c_prompt, c_sample, jnp.zeros((rows - nb_p - nb_s, D_MODEL), F32)], axis=0)
    mod = _modulation(c_all, w_ada[0], b_ada[0]).reshape(rows, 6, D_MODEL)
    modf = _modulation(c_all, w_adaf, b_adaf).reshape(rows, 2, D_MODEL)

    wr = jnp.concatenate([w_rg[0], w_re[0], jnp.zeros((D_MODEL, LANES - N_GROUPS - N_EXPERTS), F32)], axis=1)
    wrh = wr.astype(BF16)
    wrl = (wr - wrh.astype(F32)).astype(BF16)
    br = jnp.concatenate([b_rg[0], b_re[0], jnp.zeros((LANES - N_GROUPS - N_EXPERTS,), F32)]).reshape(1, LANES)
    seg = lax.broadcasted_iota(I32, (2 * LANES, LANES), 0) % LANES // HEAD_DIM
    bd = (seg == lax.broadcasted_iota(I32, (2 * LANES, LANES), 1) // HEAD_DIM).astype(BF16)
    ltri = (lax.broadcasted_iota(I32, (TM_PROJ, TM_PROJ), 1)
            < lax.broadcasted_iota(I32, (TM_PROJ, TM_PROJ), 0)).astype(BF16)
    prm = dict(
        ltri=ltri,
        ln1_g=ln1_g[0], ln2_g=ln2_g[0], w_in=w_in[0].astype(BF16), w_out=w_out[0].astype(BF16),
        qg=jnp.tile(qn_g[0], 2).reshape(1, LANES), kg=jnp.tile(kn_g[0], 2).reshape(1, LANES), bd=bd,
        on_a_g=on_a_g[0], on_b_g=on_b_g[0], wr_split=jnp.concatenate([wrh, wrl], axis=1), br=br,
        w1=w1[0].astype(BF16), w3=w3[0].astype(BF16), w2=w2[0].astype(BF16), lnf_g=lnf_g,
    )
    y_prompt = _trunk(x_prompt, 0, mod, modf, prm)
    y_sample = _trunk(x_sample, nb_p, mod, modf, prm)
    return (y_prompt, y_sample)
```

```python
import functools

import jax
import jax.numpy as jnp
from jax import lax
from jax.experimental import pallas as pl
from jax.experimental.pallas import tpu as pltpu

F32 = jnp.float32
BF16 = jnp.bfloat16
I32 = jnp.int32

D_MODEL = 1024
HEAD_DIM = 64
N_HEADS = 8
WIDTH = N_HEADS * HEAD_DIM
KV_WIDTH = 2 * HEAD_DIM
IN_COLS = 3 * WIDTH + WIDTH + 2 * KV_WIDTH
ROPE_THETA = 500000.0
ROPE_DIMS_A = 16
AXIAL_THETA = 10000.0
GRID_W = 64
N_GROUPS = 4
EXPERTS_PER_GROUP = 8
N_EXPERTS = 32
D_EXPERT = 512
EPS = 1e-6
NEG = -1e30
DIL_W = 64
A_BATCH = 8

LANES = 128
VMEM_LIMIT = 56 * 1024 * 1024

TM_PROJ = 512
TILE_A = 1024
TQ_B = 256
NQ_B = 8
B_ITEMS_PER_TRIP = 8
BF16_ROWS = 16
LOG2E = 1.4426950408889634
ROW_TILES = D_MODEL // LANES
BM_MOE = 512
DMA_UNROLL = 8
TM_FIN = 512


def _cparams(sem):
    return pltpu.CompilerParams(dimension_semantics=sem, vmem_limit_bytes=VMEM_LIMIT)


def _mod_kernel(c_ref, w_ref, b_ref, o_ref):
    c = c_ref[...]
    a = c / (1.0 + jnp.exp(-c))
    o_ref[...] = jnp.dot(a, w_ref[...], preferred_element_type=F32,
                         precision=lax.Precision.HIGHEST) + b_ref[...]


def _modulation(c_pad, w, b):
    rows, d = c_pad.shape
    n = w.shape[1]
    tn = 1024
    return pl.pallas_call(
        _mod_kernel,
        grid=(n // tn,),
        in_specs=[pl.BlockSpec((rows, d), lambda j: (0, 0)),
                  pl.BlockSpec((d, tn), lambda j: (0, j)),
                  pl.BlockSpec((1, tn), lambda j: (0, j))],
        out_specs=pl.BlockSpec((rows, tn), lambda j: (0, j)),
        out_shape=jax.ShapeDtypeStruct((rows, n), F32),
        compiler_params=_cparams(("arbitrary",)),
        name="modulation",
    )(c_pad, w, b.reshape(1, n))


def _rope_tables(seq):
    t = jnp.arange(seq)
    tf = t.astype(F32)
    inv_a = ROPE_THETA ** (-jnp.arange(0, ROPE_DIMS_A, 2, dtype=F32) / ROPE_DIMS_A)
    ang = tf[:, None] * inv_a[None, :]
    cos, sin = jnp.cos(ang), jnp.sin(ang)
    rest = HEAD_DIM - ROPE_DIMS_A
    one = jnp.ones((seq, rest), F32)
    zero = jnp.zeros((seq, rest), F32)
    z8 = jnp.zeros_like(sin)
    cos64 = jnp.concatenate([cos, cos, one], 1)
    sm64 = jnp.concatenate([-sin, z8, zero], 1)
    sp64 = jnp.concatenate([z8, sin, zero], 1)
    ta = jnp.stack([jnp.tile(a, (1, 2)) for a in (cos64, sm64, sp64)])

    n_ax = HEAD_DIM // 2
    inv_b = AXIAL_THETA ** (-jnp.arange(0, n_ax, 2, dtype=F32) / n_ax)
    row_pos = (t // GRID_W).astype(F32)
    col_pos = (t % GRID_W).astype(F32)
    ar = row_pos[:, None] * inv_b[None, :]
    ac = col_pos[:, None] * inv_b[None, :]
    z16 = jnp.zeros_like(ar)
    cos64 = jnp.concatenate([jnp.cos(ar), jnp.cos(ar), jnp.cos(ac), jnp.cos(ac)], 1)
    sm64 = jnp.concatenate([-jnp.sin(ar), z16, -jnp.sin(ac), z16], 1)
    sp64 = jnp.concatenate([z16, jnp.sin(ar), z16, jnp.sin(ac)], 1)
    tb = jnp.stack([jnp.tile(a, (1, 2)) for a in (cos64, sm64, sp64)])
    return ta, tb


def _inproj_kernel(x_ref, mod_ref, g_ref, w_ref, ta_ref, tb_ref, qg_ref, kg_ref, bd_ref,
                   qa_ref, ka_ref, va_ref, qb_ref, kb_ref, vb_ref):
    x = x_ref[...]
    mod = mod_ref[0]
    inv = lax.rsqrt(jnp.mean(x * x, axis=-1, keepdims=True) + EPS)
    h = (x * inv) * (g_ref[...] * (1.0 + mod[1:2])) + mod[0:1]
    hb = h.astype(BF16)
    cos_a, sm_a, sp_a = ta_ref[0], ta_ref[1], ta_ref[2]
    cos_b, sm_b, sp_b = tb_ref[0], tb_ref[1], tb_ref[2]
    bd = bd_ref[...]
    lane = lax.broadcasted_iota(I32, (1, LANES), 1)
    low = lane < HEAD_DIM

    def proj(lo, n):
        return jnp.dot(hb, w_ref[:, lo:lo + n], preferred_element_type=F32)

    def rope(y, cos, sm, sp, sh):
        return y * cos + pltpu.roll(y, LANES - sh, 1) * sm + pltpu.roll(y, sh, 1) * sp

    def headnorm(y, g):
        t = y * y
        thi = t.astype(BF16)
        tlo = (t - thi.astype(F32)).astype(BF16)
        ss = jnp.dot(jnp.concatenate([thi, tlo], axis=1), bd, preferred_element_type=F32)
        return y * lax.rsqrt(ss * (1.0 / HEAD_DIM) + EPS) * g

    scale = HEAD_DIM ** -0.5
    qa = proj(0, WIDTH)
    ka = proj(WIDTH, WIDTH)
    for c in range(WIDTH // LANES):
        sl = slice(c * LANES, (c + 1) * LANES)
        qa_ref[:, sl] = rope(qa[:, sl], cos_a, sm_a, sp_a, 8) * (scale * LOG2E)
        ka_ref[:, sl] = rope(ka[:, sl], cos_a, sm_a, sp_a, 8)
    va_ref[...] = proj(2 * WIDTH, WIDTH)

    qb = proj(3 * WIDTH, WIDTH)
    qg = qg_ref[...]
    for c in range(WIDTH // LANES):
        y = rope(headnorm(qb[:, c * LANES:(c + 1) * LANES], qg), cos_b, sm_b, sp_b, 16) * (scale * LOG2E)
        ysw = pltpu.roll(y, HEAD_DIM, 1)
        grp = c // 2
        h0 = y if grp == 0 else ysw
        h1 = ysw if grp == 0 else y
        keep = low if grp == 0 else jnp.logical_not(low)
        for u, hh in enumerate((h0, h1)):
            ht = jnp.where(keep, hh, 0.0).T.astype(BF16)
            for sub in range(ht.shape[1] // TQ_B):
                qb_ref[sub, 2 * c + u] = ht[:, sub * TQ_B:(sub + 1) * TQ_B]
    kb = proj(4 * WIDTH, KV_WIDTH)
    kb_ref[...] = rope(headnorm(kb, kg_ref[...]), cos_b, sm_b, sp_b, 16).astype(BF16)
    vb_ref[0] = proj(4 * WIDTH + KV_WIDTH, KV_WIDTH).T.astype(BF16)


def _inproj(x2d, mod, boff, seq, ln1_g, w_in_bf, ta, tb, qg, kg, bd):
    t_tok = x2d.shape[0]
    tm = TM_PROJ
    per_seq = seq // tm
    tok = lambda i: (i, 0)
    const = lambda i: (0, 0)
    return pl.pallas_call(
        _inproj_kernel,
        grid=(t_tok // tm,),
        in_specs=[
            pl.BlockSpec((tm, D_MODEL), tok),
            pl.BlockSpec((1, 6, D_MODEL), lambda i: (boff + i // per_seq, 0, 0)),
            pl.BlockSpec((1, D_MODEL), const),
            pl.BlockSpec((D_MODEL, IN_COLS), const),
            pl.BlockSpec((3, tm, LANES), lambda i: (0, i % per_seq, 0)),
            pl.BlockSpec((3, tm, LANES), lambda i: (0, i % per_seq, 0)),
            pl.BlockSpec((1, LANES), const),
            pl.BlockSpec((1, LANES), const),
            pl.BlockSpec((2 * LANES, LANES), const),
        ],
        out_specs=[
            pl.BlockSpec((tm, WIDTH), tok),
            pl.BlockSpec((tm, WIDTH), tok),
            pl.BlockSpec((tm, WIDTH), tok),
            pl.BlockSpec((tm // TQ_B, N_HEADS, LANES, TQ_B), lambda i: (i, 0, 0, 0)),
            pl.BlockSpec((tm, KV_WIDTH), tok),
            pl.BlockSpec((1, KV_WIDTH, tm), lambda i: (i, 0, 0)),
        ],
        out_shape=[
            jax.ShapeDtypeStruct((t_tok, WIDTH), F32),
            jax.ShapeDtypeStruct((t_tok, WIDTH), F32),
            jax.ShapeDtypeStruct((t_tok, WIDTH), F32),
            jax.ShapeDtypeStruct((t_tok // TQ_B, N_HEADS, LANES, TQ_B), BF16),
            jax.ShapeDtypeStruct((t_tok, KV_WIDTH), BF16),
            jax.ShapeDtypeStruct((t_tok // tm, KV_WIDTH, tm), BF16),
        ],
        compiler_params=_cparams(("parallel",)),
        name="inproj",
    )(x2d, mod, ln1_g.reshape(1, D_MODEL), w_in_bf, ta, tb, qg, kg, bd)


def _attn_a_kernel(q_ref, kp_ref, kc_ref, kn_ref, vp_ref, vc_ref, vn_ref, o_ref,
                   m1, l1, a1, m2, l2, a2, m3, l3, a3, *, seq):
    n = pl.program_id(1)
    t = TILE_A
    w = DIL_W

    lane = lax.broadcasted_iota(I32, (1, LANES), 1)
    low = lane < HEAD_DIM
    row_i = lax.broadcasted_iota(I32, (2 * w, 1), 0) & (w - 1)
    col_m = lax.broadcasted_iota(I32, (1, 3 * w), 1)
    band_bias = jnp.where(jnp.abs(col_m - w - row_i) <= w, 0.0, NEG)
    base = n * t - t

    def window_rows(refs, start, count, d):
        parts = []
        while count > 0:
            ref = refs[start // t]
            local = start % t
            take = min(count, -(-(t - local) // d))
            parts.append(ref[pl.ds(local, take, stride=d), :] if d > 1 else ref[pl.ds(local, take), :])
            start += take * d
            count -= take
        return parts[0] if len(parts) == 1 else jnp.concatenate(parts, axis=0)

    units = []
    for d, m_s, l_s, a_s in ((1, m1, l1, a1), (4, m2, l2, a2), (16, m3, l3, a3)):
        for j in range(t // w):
            qs = (j // d) * (w * d) + (j % d)
            units.append((d, qs, m_s, l_s, a_s))

    def rows(qs, d):
        return pl.ds(qs, w, stride=d) if d > 1 else pl.ds(qs, w)

    def score(unit):
        d, qs = unit[0], unit[1]
        ks = qs + t - w * d
        q = q_ref[rows(qs, d), :]
        k = window_rows((kp_ref, kc_ref, kn_ref), ks, 3 * w, d).astype(BF16)
        q2 = jnp.concatenate([jnp.where(low, q, 0.0), jnp.where(low, 0.0, q)], axis=0).astype(BF16)
        s = lax.dot_general(q2, k, (((1,), (1,)), ((), ())), preferred_element_type=F32) + band_bias
        if ks >= t and ks + (3 * w - 1) * d < 2 * t:
            return s
        kpos = base + ks + d * col_m
        edge_bias = jnp.where(kpos >= 0, jnp.where(kpos < seq, 0.0, NEG), NEG)
        return s + edge_bias

    def finish(unit, s):
        d, qs, m_s, l_s, a_s = unit
        ks = qs + t - w * d
        v = window_rows((vp_ref, vc_ref, vn_ref), ks, 3 * w, d).astype(BF16)
        mx = jnp.max(s, axis=-1, keepdims=True)
        p = jnp.exp2(s - mx)
        den = jnp.sum(p, axis=-1, keepdims=True)
        o = jnp.dot(p.astype(BF16), v, preferred_element_type=F32)
        m_s[rows(qs, d), :] = jnp.where(low, mx[0:w], mx[w:2 * w])
        l_s[rows(qs, d), :] = jnp.where(low, den[0:w], den[w:2 * w])
        a_s[rows(qs, d), :] = jnp.where(low, o[0:w], o[w:2 * w])

    nb = A_BATCH
    pending = [score(u) for u in units[:nb]]
    for b0 in range(0, len(units), nb):
        ahead = [score(u) for u in units[b0 + nb:b0 + 2 * nb]]
        for u, s in zip(units[b0:b0 + nb], pending):
            finish(u, s)
        pending = ahead

    mm = jnp.maximum(jnp.maximum(m1[...], m2[...]), m3[...])
    w1 = jnp.exp2(m1[...] - mm)
    w2 = jnp.exp2(m2[...] - mm)
    w3 = jnp.exp2(m3[...] - mm)
    num = w1 * a1[...] + w2 * a2[...] + w3 * a3[...]
    den = w1 * l1[...] + w2 * l2[...] + w3 * l3[...]
    o_ref[...] = (num / den).astype(o_ref.dtype)


def _attn_a(qa, ka, va, batch, seq):
    t = TILE_A
    nt = seq // t
    q3 = qa.reshape(batch, seq, WIDTH)
    k3 = ka.reshape(batch, seq, WIDTH)
    v3 = va.reshape(batch, seq, WIDTH)
    cur = lambda b, n, p: (b, n, p)
    prev = lambda b, n, p: (b, jnp.maximum(n - 1, 0), p)
    nxt = lambda b, n, p: (b, jnp.minimum(n + 1, nt - 1), p)
    blk = (None, t, LANES)
    scr = pltpu.VMEM((t, LANES), F32)
    out = pl.pallas_call(
        functools.partial(_attn_a_kernel, seq=seq),
        grid=(batch, nt, WIDTH // LANES),
        in_specs=[pl.BlockSpec(blk, cur),
                  pl.BlockSpec(blk, prev), pl.BlockSpec(blk, cur), pl.BlockSpec(blk, nxt),
                  pl.BlockSpec(blk, prev), pl.BlockSpec(blk, cur), pl.BlockSpec(blk, nxt)],
        out_specs=pl.BlockSpec(blk, cur),
        out_shape=jax.ShapeDtypeStruct((batch, seq, WIDTH), BF16),
        scratch_shapes=[scr] * 9,
        compiler_params=_cparams(("parallel", "parallel", "parallel")),
        name="attn_dilated",
    )(q3, k3, k3, k3, v3, v3, v3)
    return out.reshape(batch * seq, WIDTH)


def _attn_b_kernel(qt_ref, k_ref, vt_ref, o_ref, m_s, acc_s, s_a, s_b, *, nkv):
    nqt, tq = qt_ref.shape[0], qt_ref.shape[3]
    tk = k_ref.shape[1]
    m_s[...] = jnp.full(m_s.shape, NEG, F32)
    acc_s[...] = jnp.zeros(acc_s.shape, F32)
    fold = 8
    ones_rows = jnp.ones((BF16_ROWS, tk), BF16)
    log_nkv = nkv.bit_length() - 1

    def score(t, dst, h):
        dst[h] = jnp.dot(k_ref[t & (nkv - 1)], qt_ref[t >> log_nkv, h], preferred_element_type=F32)

    def update(t, src, h):
        qi, j = t >> log_nkv, t & (nkv - 1)
        cs = slice(tq * h, tq * (h + 1))
        r0 = HEAD_DIM * (h // (N_HEADS // 2))
        vt = jnp.concatenate([vt_ref[j, r0:r0 + HEAD_DIM, :], ones_rows], axis=0)
        s = src[h]
        part = jnp.max(s.reshape(fold, tk // fold, tq), axis=0)
        m_old = m_s[qi, :, cs]
        m_new = jnp.maximum(m_old, jnp.max(part, axis=0, keepdims=True))
        alpha = jnp.exp2(m_old - m_new)
        p = jnp.exp2((s - m_new).astype(BF16))
        acc_s[qi, :, cs] = alpha * acc_s[qi, :, cs] + jnp.dot(vt, p, preferred_element_type=F32)
        m_s[qi, :, cs] = m_new

    def step(t_next, dst, t_cur, src):
        for h in range(N_HEADS):
            if t_next is not None:
                score(t_next, dst, h)
            update(t_cur, src, h)

    n_items = nqt * nkv
    for h in range(N_HEADS):
        score(0, s_a, h)

    bufs = (s_a, s_b)
    per_trip = B_ITEMS_PER_TRIP

    def body(trip, carry):
        t0 = per_trip * trip
        for u in range(per_trip):
            step(t0 + u + 1, bufs[(u + 1) % 2], t0 + u, bufs[u % 2])
        return carry

    lax.fori_loop(0, n_items // per_trip - 1, body, 0)
    t0 = n_items - per_trip
    for u in range(per_trip - 1):
        step(t0 + u + 1, bufs[(u + 1) % 2], t0 + u, bufs[u % 2])
    step(None, None, n_items - 1, bufs[(per_trip - 1) % 2])

    for qi in range(nqt):
        o = acc_s[qi, 0:HEAD_DIM] / acc_s[qi, HEAD_DIM:HEAD_DIM + 1]
        for c in range(N_HEADS // 2):
            pair = jnp.concatenate([o[:, (2 * c) * tq:(2 * c + 1) * tq],
                                    o[:, (2 * c + 1) * tq:(2 * c + 2) * tq]], axis=0)
            o_ref[qi * tq:(qi + 1) * tq, c * LANES:(c + 1) * LANES] = pair.T.astype(o_ref.dtype)


def _attn_b(qbt, kb, vbt, batch, seq):
    tq, tk, nqt = TQ_B, TM_PROJ, NQ_B
    steps = seq // (tq * nqt)
    nkv = seq // tk
    assert nkv & (nkv - 1) == 0 and nkv >= 2
    sbuf = pltpu.VMEM((N_HEADS, tk, tq), F32)
    k3 = kb.reshape(batch * nkv, tk, KV_WIDTH)
    return pl.pallas_call(
        functools.partial(_attn_b_kernel, nkv=nkv),
        grid=(batch, steps),
        in_specs=[pl.BlockSpec((nqt, N_HEADS, LANES, tq), lambda b, i: (b * steps + i, 0, 0, 0)),
                  pl.BlockSpec((nkv, tk, KV_WIDTH), lambda b, i: (b, 0, 0)),
                  pl.BlockSpec((nkv, KV_WIDTH, tk), lambda b, i: (b, 0, 0))],
        out_specs=pl.BlockSpec((nqt * tq, WIDTH), lambda b, i: (b * steps + i, 0)),
        out_shape=jax.ShapeDtypeStruct((batch * seq, WIDTH), BF16),
        scratch_shapes=[pltpu.VMEM((nqt, 1, N_HEADS * tq), F32),
                        pltpu.VMEM((nqt, HEAD_DIM + BF16_ROWS, N_HEADS * tq), F32), sbuf, sbuf],
        compiler_params=_cparams(("parallel", "parallel")),
        name="attn_dense",
    )(qbt, k3, vbt)


def _outproj_kernel(x_ref, oa_ref, ob_ref, mod_ref, ga_ref, gb_ref, wo_ref, ln2_ref,
                    wr_ref, br_ref, ltri_ref, x1_ref, h2_ref, rt_ref, cnt_ref):
    def rms(o, g):
        return o * lax.rsqrt(jnp.mean(o * o, axis=-1, keepdims=True) + EPS) * g

    oa = rms(oa_ref[...].astype(F32), ga_ref[...])
    ob = rms(ob_ref[...].astype(F32), gb_ref[...])
    o = jnp.concatenate([oa, ob], axis=1).astype(BF16)
    mod = mod_ref[0]
    x1 = x_ref[...] + mod[2:3] * jnp.dot(o, wo_ref[...], preferred_element_type=F32)
    x1_ref[...] = x1
    inv = lax.rsqrt(jnp.mean(x1 * x1, axis=-1, keepdims=True) + EPS)
    h2 = (x1 * inv) * (ln2_ref[...] * (1.0 + mod[4:5])) + mod[3:4]
    tm = h2.shape[0]
    for c in range(ROW_TILES):
        h2_ref[pl.ds(c, tm, stride=ROW_TILES), :] = h2[:, c * LANES:(c + 1) * LANES]

    hh = h2.astype(BF16)
    hl = (h2 - hh.astype(F32)).astype(BF16)
    both = jnp.dot(hh, wr_ref[...], preferred_element_type=F32)
    logits = (both[:, 0:LANES] + both[:, LANES:2 * LANES]
              + jnp.dot(hl, wr_ref[:, 0:LANES], preferred_element_type=F32)) + br_ref[...]

    lane = lax.broadcasted_iota(I32, logits.shape, 1).astype(F32)
    big = jnp.float32(LANES)
    is_g = lane < N_GROUPS
    gl = jnp.where(is_g, logits, -jnp.inf)
    gmax = jnp.max(gl, axis=-1, keepdims=True)
    g_sel = jnp.min(jnp.where(gl == gmax, lane, big), axis=-1, keepdims=True)
    g_den = jnp.sum(jnp.exp(gl - gmax), axis=-1, keepdims=True)
    g_w = 1.0 / g_den
    e_lo = N_GROUPS + EXPERTS_PER_GROUP * g_sel
    is_e = (lane >= e_lo) & (lane < e_lo + EXPERTS_PER_GROUP)
    el = jnp.where(is_e, logits, -jnp.inf)
    emax = jnp.max(el, axis=-1, keepdims=True)
    p = jnp.where(is_e, jnp.exp(el - emax), -1.0)
    p1 = jnp.max(p, axis=-1, keepdims=True)
    i1 = jnp.min(jnp.where(p == p1, lane, big), axis=-1, keepdims=True)
    pm = jnp.where(lane == i1, -1.0, p)
    p2 = jnp.max(pm, axis=-1, keepdims=True)
    i2 = jnp.min(jnp.where(pm == p2, lane, big), axis=-1, keepdims=True)
    tot = p1 + p2
    gate1 = g_w * p1 / tot
    gate2 = g_w * p2 / tot
    e1 = i1 - N_GROUPS
    e2 = i2 - N_GROUPS
    memb = jnp.where(lane == e1, 1.0, jnp.where(lane == e2, 1.0, 0.0))
    before = jnp.dot(ltri_ref[...], memb.astype(BF16), preferred_element_type=F32)
    r1 = jnp.sum(jnp.where(lane == e1, before, 0.0), axis=-1, keepdims=True)
    r2 = jnp.sum(jnp.where(lane == e2, before, 0.0), axis=-1, keepdims=True)
    cnt_ref[0] = jnp.broadcast_to(jnp.sum(memb, axis=0, keepdims=True), cnt_ref.shape[1:])
    cols = (gate1, gate2, e1, e2, r1, r2)
    out = jnp.zeros_like(logits)
    for idx, col in enumerate(cols):
        out = jnp.where(lane == idx, col, out)
    rt_ref[...] = out


def _outproj(x2d, oa, ob, mod, boff, seq, ga, gb, wo_bf, ln2_g, wr_split, br, ltri):
    t_tok = x2d.shape[0]
    tm = TM_PROJ
    per_seq = seq // tm
    tok = lambda i: (i, 0)
    const = lambda i: (0, 0)
    return pl.pallas_call(
        _outproj_kernel,
        grid=(t_tok // tm,),
        in_specs=[
            pl.BlockSpec((tm, D_MODEL), tok),
            pl.BlockSpec((tm, WIDTH), tok),
            pl.BlockSpec((tm, WIDTH), tok),
            pl.BlockSpec((1, 6, D_MODEL), lambda i: (boff + i // per_seq, 0, 0)),
            pl.BlockSpec((1, WIDTH), const),
            pl.BlockSpec((1, WIDTH), const),
            pl.BlockSpec((D_MODEL, D_MODEL), const),
            pl.BlockSpec((1, D_MODEL), const),
            pl.BlockSpec((D_MODEL, 2 * LANES), const),
            pl.BlockSpec((1, LANES), const),
            pl.BlockSpec((tm, tm), const),
        ],
        out_specs=[pl.BlockSpec((tm, D_MODEL), tok),
                   pl.BlockSpec((tm * ROW_TILES, LANES), tok),
                   pl.BlockSpec((tm, LANES), tok),
                   pl.BlockSpec((1, 8, LANES), lambda i: (i, 0, 0))],
        out_shape=[jax.ShapeDtypeStruct((t_tok, D_MODEL), F32),
                   jax.ShapeDtypeStruct((t_tok * ROW_TILES, LANES), F32),
                   jax.ShapeDtypeStruct((t_tok, LANES), F32),
                   jax.ShapeDtypeStruct((t_tok // tm, 8, LANES), F32)],
        compiler_params=_cparams(("parallel",)),
        name="outproj_router",
    )(x2d, oa, ob, mod, ga.reshape(1, WIDTH), gb.reshape(1, WIDTH), wo_bf,
      ln2_g.reshape(1, D_MODEL), wr_split, br, ltri)


def _dispatch_kernel(fill_ref, has_ref, nu_ref, idx_ref, h2_ref, out_hbm, zbuf, sem, fsem):
    i = pl.program_id(0)
    n_asg = idx_ref.shape[2]
    rt = ROW_TILES

    def fill_copy(e):
        start = pl.multiple_of(fill_ref[e] * rt, rt)
        return pltpu.make_async_copy(zbuf, out_hbm.at[pl.ds(start, zbuf.shape[0])], fsem)

    def tail_copy(b):
        start = pl.multiple_of(b * zbuf.shape[0], zbuf.shape[0])
        return pltpu.make_async_copy(zbuf, out_hbm.at[pl.ds(start, zbuf.shape[0])], fsem)

    @pl.when(i == 0)
    def _():
        zbuf[...] = jnp.zeros(zbuf.shape, F32)
        for e in range(N_EXPERTS):
            @pl.when(has_ref[e] > 0)
            def _():
                fill_copy(e).start()
        for e in range(N_EXPERTS):
            @pl.when(has_ref[e] > 0)
            def _():
                fill_copy(e).wait()
        n_blocks = out_hbm.shape[0] // zbuf.shape[0]

        def tail(b, carry):
            tail_copy(b).start()
            tail_copy(b).wait()
            return carry
        lax.fori_loop(nu_ref[0], n_blocks, tail, 0)

    trips = n_asg // DMA_UNROLL
    for half in range(2):
        def body(r8, carry, half=half):
            for u in range(DMA_UNROLL):
                r = r8 * DMA_UNROLL + u
                src = pl.multiple_of((r >> 1) * rt, rt)
                dst = pl.multiple_of(idx_ref[0, 0, r] * rt, rt)
                pltpu.make_async_copy(h2_ref.at[pl.ds(src, rt)], out_hbm.at[pl.ds(dst, rt)],
                                      sem.at[half]).start(priority=u % 2)
            return carry
        lax.fori_loop(half * (trips // 2), (half + 1) * (trips // 2), body, 0)
    for half in range(2):
        pltpu.make_async_copy(h2_ref, out_hbm.at[pl.ds(0, h2_ref.shape[0])], sem.at[half]).wait()


def _dispatch(h2, slot_of, n_slots, fill_at, has_fill, n_used):
    tm = TM_PROJ
    n_steps = h2.shape[0] // (tm * ROW_TILES)
    grid_spec = pltpu.PrefetchScalarGridSpec(
        num_scalar_prefetch=3,
        grid=(n_steps,),
        in_specs=[pl.BlockSpec((1, 1, 2 * tm), lambda i, fa, hf, nu: (i, 0, 0), memory_space=pltpu.SMEM),
                  pl.BlockSpec((tm * ROW_TILES, LANES), lambda i, fa, hf, nu: (i, 0))],
        out_specs=pl.BlockSpec(memory_space=pl.ANY),
        scratch_shapes=[pltpu.VMEM((BM_MOE * ROW_TILES, LANES), F32),
                        pltpu.SemaphoreType.DMA((2,)), pltpu.SemaphoreType.DMA(())],
    )
    return pl.pallas_call(
        _dispatch_kernel,
        grid_spec=grid_spec,
        out_shape=jax.ShapeDtypeStruct((n_slots * ROW_TILES, LANES), F32),
        compiler_params=_cparams(("arbitrary",)),
        name="moe_dispatch",
    )(fill_at, has_fill, n_used, slot_of.reshape(n_steps, 1, 2 * tm), h2)


def _moe_kernel(be_ref, nu_ref, x_ref, w1_ref, w3_ref, w2_ref, y_ref):
    del be_ref
    bm = BM_MOE
    rt = ROW_TILES

    @pl.when(pl.program_id(0) < nu_ref[0])
    def _():
        x = jnp.concatenate([x_ref[pl.ds(c, bm, stride=rt), :] for c in range(rt)], axis=1).astype(BF16)
        a = jnp.dot(x, w1_ref[...], preferred_element_type=F32)
        b = jnp.dot(x, w3_ref[...], preferred_element_type=F32)
        hmid = ((a / (1.0 + jnp.exp(-a))) * b).astype(BF16)
        y = jnp.dot(hmid, w2_ref[...], preferred_element_type=F32)
        for c in range(rt):
            y_ref[pl.ds(c, bm, stride=rt), :] = y[:, c * LANES:(c + 1) * LANES]

    @pl.when(pl.program_id(0) >= nu_ref[0])
    def _():
        y_ref[...] = jnp.zeros(y_ref.shape, F32)


def _moe(xs, block_e, n_used, w1_bf, w3_bf, w2_bf):
    bm = BM_MOE
    n_blocks = block_e.shape[0]
    row_in = pl.BlockSpec((bm * ROW_TILES, LANES), lambda i, be, nu: (jnp.minimum(i, nu[0] - 1), 0))
    row_out = pl.BlockSpec((bm * ROW_TILES, LANES), lambda i, be, nu: (i, 0))
    grid_spec = pltpu.PrefetchScalarGridSpec(
        num_scalar_prefetch=2,
        grid=(n_blocks,),
        in_specs=[
            row_in,
            pl.BlockSpec((None, D_MODEL, D_EXPERT), lambda i, be, nu: (be[i], 0, 0)),
            pl.BlockSpec((None, D_MODEL, D_EXPERT), lambda i, be, nu: (be[i], 0, 0)),
            pl.BlockSpec((None, D_EXPERT, D_MODEL), lambda i, be, nu: (be[i], 0, 0)),
        ],
        out_specs=row_out,
    )
    return pl.pallas_call(
        _moe_kernel,
        grid_spec=grid_spec,
        out_shape=jax.ShapeDtypeStruct(xs.shape, F32),
        compiler_params=_cparams(("arbitrary",)),
        name="moe_experts",
    )(block_e, n_used, xs, w1_bf, w3_bf, w2_bf)


def _moe_plan(rt, cnt, tm, bm):
    t_tok = rt.shape[0]
    expert = rt[:, 2:4].astype(I32)
    rank_in_tile = rt[:, 4:6].astype(I32)
    tile_cnt = cnt[:, 0, :N_EXPERTS].astype(I32)
    tile_off = jnp.cumsum(tile_cnt, axis=0) - tile_cnt
    counts = jnp.sum(tile_cnt, axis=0)
    pcounts = (counts + bm - 1) // bm * bm
    pends = jnp.cumsum(pcounts)
    pstarts = pends - pcounts
    base = jnp.repeat(pstarts[None, :] + tile_off, tm, axis=0)
    onehot = expert[:, :, None] == jnp.arange(N_EXPERTS, dtype=I32)[None, None, :]
    dest = jnp.sum(jnp.where(onehot, base[:, None, :], 0), axis=-1) + rank_in_tile
    n_blocks = 2 * t_tok // bm + N_EXPERTS
    blk_start = jnp.arange(n_blocks, dtype=I32) * bm
    block_e = jnp.minimum(jnp.sum((pends[None, :] <= blk_start[:, None]).astype(I32), axis=1), N_EXPERTS - 1)
    n_used = (pends[-1] // bm).astype(I32).reshape(1)
    fill_at = jnp.maximum(pends - bm, 0).astype(I32)
    has_fill = (pcounts > 0).astype(I32)
    return dest.reshape(-1).astype(I32), block_e.astype(I32), n_used, fill_at, has_fill


def _final_kernel(idxc_ref, idxn_ref, x1_ref, ys_hbm, rt_ref, mod_ref, modf_ref, g_ref, o_ref, ybuf, sem):
    i = pl.program_id(0)
    n = pl.num_programs(0)
    slot = i % 2
    rt = rt_ref[...]
    tm = rt.shape[0]
    n_asg = 2 * tm
    rtl = ROW_TILES

    def start_gather(idx_ref, dst_slot):
        def body(r8, carry):
            for u in range(DMA_UNROLL):
                r = r8 * DMA_UNROLL + u
                src = pl.multiple_of(idx_ref[0, 0, r] * rtl, rtl)
                pltpu.make_async_copy(ys_hbm.at[pl.ds(src, rtl)],
                                      ybuf.at[dst_slot, pl.ds(pl.multiple_of(r * rtl, rtl), rtl)],
                                      sem.at[dst_slot]).start(priority=u % 2)
            return carry
        lax.fori_loop(0, n_asg // DMA_UNROLL, body, 0)

    @pl.when(i == 0)
    def _():
        start_gather(idxc_ref, 0)

    @pl.when(i + 1 < n)
    def _():
        start_gather(idxn_ref, 1 - slot)

    pltpu.make_async_copy(ys_hbm.at[pl.ds(0, n_asg * rtl)], ybuf.at[slot], sem.at[slot]).wait()

    g1, g2 = rt[:, 0:1], rt[:, 1:2]
    moe = jnp.concatenate(
        [g1 * ybuf[slot, pl.ds(c, tm, stride=2 * rtl), :]
         + g2 * ybuf[slot, pl.ds(rtl + c, tm, stride=2 * rtl), :] for c in range(rtl)], axis=1)
    x2 = x1_ref[...] + mod_ref[0][5:6] * moe
    modf = modf_ref[0]
    inv = lax.rsqrt(jnp.mean(x2 * x2, axis=-1, keepdims=True) + EPS)
    o_ref[...] = (x2 * inv) * (g_ref[...] * (1.0 + modf[1:2])) + modf[0:1]


def _final(x1, ys, slot_of, rt, mod, modf, boff, seq, lnf_g):
    t_tok = x1.shape[0]
    tm = TM_FIN
    n_steps = t_tok // tm
    per_seq = seq // tm
    tok = lambda i: (i, 0)
    idx3 = slot_of.reshape(n_steps, 1, 2 * tm)
    smem_blk = lambda fn: pl.BlockSpec((1, 1, 2 * tm), fn, memory_space=pltpu.SMEM)
    return pl.pallas_call(
        _final_kernel,
        grid=(n_steps,),
        in_specs=[smem_blk(lambda i: (i, 0, 0)),
                  smem_blk(lambda i: (jnp.minimum(i + 1, n_steps - 1), 0, 0)),
                  pl.BlockSpec((tm, D_MODEL), tok),
                  pl.BlockSpec(memory_space=pl.ANY),
                  pl.BlockSpec((tm, LANES), tok),
                  pl.BlockSpec((1, 6, D_MODEL), lambda i: (boff + i // per_seq, 0, 0)),
                  pl.BlockSpec((1, 2, D_MODEL), lambda i: (boff + i // per_seq, 0, 0)),
                  pl.BlockSpec((1, D_MODEL), lambda i: (0, 0))],
        out_specs=pl.BlockSpec((tm, D_MODEL), tok),
        out_shape=jax.ShapeDtypeStruct((t_tok, D_MODEL), F32),
        scratch_shapes=[pltpu.VMEM((2, tm * 2 * ROW_TILES, LANES), F32), pltpu.SemaphoreType.DMA((2,))],
        compiler_params=_cparams(("arbitrary",)),
        name="combine_final",
    )(idx3, idx3, x1, ys, rt, mod, modf, lnf_g.reshape(1, D_MODEL))


def _trunk(x, boff, mod, modf, prm):
    batch, seq, _ = x.shape
    x2d = x.reshape(batch * seq, D_MODEL)
    ta, tb = _rope_tables(seq)
    qa, ka, va, qb8, kb, vb = _inproj(x2d, mod, boff, seq, prm["ln1_g"], prm["w_in"], ta, tb,
                                      prm["qg"], prm["kg"], prm["bd"])
    oa = _attn_a(qa, ka, va, batch, seq)
    ob = _attn_b(qb8, kb, vb, batch, seq)
    x1, h2, rt, cnt = _outproj(x2d, oa, ob, mod, boff, seq, prm["on_a_g"], prm["on_b_g"], prm["w_out"],
                               prm["ln2_g"], prm["wr_split"], prm["br"], prm["ltri"])
    dest, block_e, n_used, fill_at, has_fill = _moe_plan(rt, cnt, TM_PROJ, BM_MOE)
    xs = _dispatch(h2, dest, block_e.shape[0] * BM_MOE, fill_at, has_fill, n_used)
    ys = _moe(xs, block_e, n_used, prm["w1"], prm["w3"], prm["w2"])
    y = _final(x1, ys, dest, rt, mod, modf, boff, seq, prm["lnf_g"])
    return y.reshape(batch, seq, D_MODEL)


def kernel(x_prompt, x_sample, c_prompt, c_sample, ln1_g, ln2_g, w_ada, b_ada, w_in, w_out, qn_g, kn_g,
           on_a_g, on_b_g, w_rg, b_rg, w_re, b_re, w1, w3, w2, lnf_g, w_adaf, b_adaf):
    assert ln1_g.shape[0] == 1, "single-layer trunk"
    nb_p, nb_s = c_prompt.shape[0], c_sample.shape[0]
    rows = -(-(nb_p + nb_s) // 8) * 8
    c_all = jnp.concatenate([c_prompt, c_sample, jnp.zeros((rows - nb_p - nb_s, D_MODEL), F32)], axis=0)
    mod = _modulation(c_all, w_ada[0], b_ada[0]).reshape(rows, 6, D_MODEL)
    modf = _modulation(c_all, w_adaf, b_adaf).reshape(rows, 2, D_MODEL)

    wr = jnp.concatenate([w_rg[0], w_re[0], jnp.zeros((D_MODEL, LANES - N_GROUPS - N_EXPERTS), F32)], axis=1)
    wrh = wr.astype(BF16)
    wrl = (wr - wrh.astype(F32)).astype(BF16)
    br = jnp.concatenate([b_rg[0], b_re[0], jnp.zeros((LANES - N_GROUPS - N_EXPERTS,), F32)]).reshape(1, LANES)
    seg = lax.broadcasted_iota(I32, (2 * LANES, LANES), 0) % LANES // HEAD_DIM
    bd = (seg == lax.broadcasted_iota(I32, (2 * LANES, LANES), 1) // HEAD_DIM).astype(BF16)
    ltri = (lax.broadcasted_iota(I32, (TM_PROJ, TM_PROJ), 1)
            < lax.broadcasted_iota(I32, (TM_PROJ, TM_PROJ), 0)).astype(BF16)
    prm = dict(
        ltri=ltri,
        ln1_g=ln1_g[0], ln2_g=ln2_g[0], w_in=w_in[0].astype(BF16), w_out=w_out[0].astype(BF16),
        qg=jnp.tile(qn_g[0], 2).reshape(1, LANES), kg=jnp.tile(kn_g[0], 2).reshape(1, LANES), bd=bd,
        on_a_g=on_a_g[0], on_b_g=on_b_g[0], wr_split=jnp.concatenate([wrh, wrl], axis=1), br=br,
        w1=w1[0].astype(BF16), w3=w3[0].astype(BF16), w2=w2[0].astype(BF16), lnf_g=lnf_g,
    )
    y_prompt = _trunk(x_prompt, 0, mod, modf, prm)
    y_sample = _trunk(x_sample, nb_p, mod, modf, prm)
    return (y_prompt, y_sample)
```
